```python
import jax, jax.numpy as jnp
from jax import lax
import numpy as np

D_MODEL = 1024
BATCH = 8
SEQ = 2048
DEPTH = 2
DEC_BATCH = 128
DEC_SEQ = 8
PAST_LEN = 16384
PAGE_SIZE = 128

HEAD_DIM = 64
N_HEADS = 8
N_KV_HEADS = 2
GROUP = N_HEADS // N_KV_HEADS
ATTN_DIM = N_HEADS * HEAD_DIM
KV_DIM = N_KV_HEADS * HEAD_DIM
WINDOW = 128
ATTN_SCALE = HEAD_DIM ** -0.5
CONV_DIM = 256
CONV_WIDTH = 3
N_MEM = 256
MEM_HEADS = 4
MEM_DIM = MEM_HEADS * HEAD_DIM
N_BRANCH = 3
IN_DIM = ATTN_DIM + 2 * KV_DIM + 3 * CONV_DIM + MEM_DIM + N_BRANCH * D_MODEL
D_FF = 2816
N_EXPERTS = 8
TOP_K = 2
N_DENSE = (DEPTH + 1) // 2
N_MOE = DEPTH // 2
ALPHA = (2 * DEPTH) ** 0.25
BETA = (8 * DEPTH) ** -0.25
LN_EPS = 1e-5

kernel_name = "hybrid_swa_sink_shortconv_memxattn_moe_deepnorm_step"


def _layer_norm(x, g, b):
    xf = x.astype(jnp.float32)
    mu = jnp.mean(xf, axis=-1, keepdims=True)
    var = jnp.mean(jnp.square(xf - mu), axis=-1, keepdims=True)
    y = (xf - mu) * lax.rsqrt(var + LN_EPS) * g.astype(jnp.float32) + b.astype(jnp.float32)
    return y.astype(x.dtype)


def _split_in(z):
    sizes = (ATTN_DIM, KV_DIM, KV_DIM, CONV_DIM, CONV_DIM, CONV_DIM, MEM_DIM,
             D_MODEL, D_MODEL, D_MODEL)
    idx = np.cumsum(sizes)[:-1].tolist()
    return jnp.split(z, idx, axis=-1)


def _sink_softmax(s, sink):
    m = jnp.maximum(jnp.max(s, axis=-1, keepdims=True), sink)
    e = jnp.exp(s - m)
    return e / (jnp.sum(e, axis=-1, keepdims=True) + jnp.exp(sink - m))


def _window_attn_prompt(q, k, v, sink):
    b, s = q.shape[:2]
    nb = s // WINDOW
    qb = q.reshape(b, nb, WINDOW, N_KV_HEADS, GROUP, HEAD_DIM)
    kb = k.reshape(b, nb, WINDOW, N_KV_HEADS, HEAD_DIM)
    vb = v.reshape(b, nb, WINDOW, N_KV_HEADS, HEAD_DIM)

    def band(t):
        prev = jnp.concatenate([jnp.zeros_like(t[:, :1]), t[:, :-1]], axis=1)
        return jnp.concatenate([prev, t], axis=2)

    kband, vband = band(kb), band(vb)
    sc = jnp.einsum('bnqkgd,bnjkd->bnkgqj', qb, kband,
                    preferred_element_type=jnp.float32) * ATTN_SCALE
    qi = jnp.arange(WINDOW)[:, None]
    kj = jnp.arange(2 * WINDOW)[None, :]
    dist = qi + WINDOW - kj
    kpos = jnp.arange(nb)[:, None, None] * WINDOW + kj - WINDOW
    mask = (dist >= 0) & (dist < WINDOW) & (kpos >= 0)
    sc = jnp.where(mask[:, None, None], sc, -jnp.inf)
    p = _sink_softmax(sc, sink.astype(jnp.float32).reshape(N_KV_HEADS, GROUP, 1, 1))
    o = jnp.einsum('bnkgqj,bnjkd->bnqkgd', p.astype(v.dtype), vband)
    tail = min(WINDOW, s)
    return o.reshape(b, s, ATTN_DIM), k[:, s - tail:], v[:, s - tail:]


def _window_attn_sample(q, k, v, buf_k, buf_v, sink):
    b, t = q.shape[:2]
    nbuf = buf_k.shape[1]
    k_all = jnp.concatenate([buf_k, k], axis=1)
    v_all = jnp.concatenate([buf_v, v], axis=1)
    qg = q.reshape(b, t, N_KV_HEADS, GROUP, HEAD_DIM)
    sc = jnp.einsum('btkgd,bjkd->bkgtj', qg, k_all,
                    preferred_element_type=jnp.float32) * ATTN_SCALE
    dist = jnp.arange(t)[:, None] + nbuf - jnp.arange(nbuf + t)[None, :]
    mask = (dist >= 0) & (dist < WINDOW)
    sc = jnp.where(mask, sc, -jnp.inf)
    p = _sink_softmax(sc, sink.astype(jnp.float32).reshape(N_KV_HEADS, GROUP, 1, 1))
    o = jnp.einsum('bkgtj,bjkd->btkgd', p.astype(v.dtype), v_all)
    return o.reshape(b, t, ATTN_DIM), k_all[:, t:], v_all[:, t:]


def _short_conv(u_ext, w):
    t = u_ext.shape[1] - (CONV_WIDTH - 1)
    out = w[0] * u_ext[:, 0:t]
    for j in range(1, CONV_WIDTH):
        out = out + w[j] * u_ext[:, j:j + t]
    return out


def _mem_attn(qm, km, vm):
    sc = jnp.einsum('bshd,bmhd->bhsm', qm, km, preferred_element_type=jnp.float32) * ATTN_SCALE
    p = jax.nn.softmax(sc, axis=-1)
    o = jnp.einsum('bhsm,bmhd->bshd', p.astype(vm.dtype), vm)
    return o.reshape(qm.shape[0], qm.shape[1], MEM_DIM)


def _mixer(x, w_in, conv_w, sink, p_attn, p_conv, p_mem, w_o, mem_k, mem_v,
           buf_k, buf_v, buf_conv):
    b, t = x.shape[:2]
    q, k, v, h, g_b, g_c, qm, gate_a, gate_c, gate_m = _split_in(x @ w_in)
    q = q.reshape(b, t, N_HEADS, HEAD_DIM)
    k = k.reshape(b, t, N_KV_HEADS, HEAD_DIM)
    v = v.reshape(b, t, N_KV_HEADS, HEAD_DIM)
    if buf_k is None:
        o_a, nk, nv = _window_attn_prompt(q, k, v, sink)
        u_prev = jnp.zeros((b, CONV_WIDTH - 1, CONV_DIM), x.dtype)
    else:
        o_a, nk, nv = _window_attn_sample(q, k, v, buf_k, buf_v, sink)
        u_prev = buf_conv
    u_ext = jnp.concatenate([u_prev, g_c * h], axis=1)
    o_c = g_b * _short_conv(u_ext, conv_w)
    o_m = _mem_attn(qm.reshape(b, t, MEM_HEADS, HEAD_DIM), mem_k, mem_v)
    merged = (jax.nn.sigmoid(gate_a) * (o_a @ p_attn)
              + jax.nn.sigmoid(gate_c) * (o_c @ p_conv)
              + jax.nn.sigmoid(gate_m) * (o_m @ p_mem))
    new_conv = u_ext[:, u_ext.shape[1] - (CONV_WIDTH - 1):]
    return merged @ w_o, nk, nv, new_conv


def _swiglu(x, wg, wu, wd):
    return (jax.nn.silu(x @ wg) * (x @ wu)) @ wd


def _moe(x, w_router, b_router, wg, wu, wd):
    shp = x.shape
    xt = x.reshape(-1, D_MODEL)
    logits = jnp.dot(xt, w_router, preferred_element_type=jnp.float32) + b_router.astype(jnp.float32)
    top_v, top_i = lax.top_k(logits, TOP_K)
    top_w = jax.nn.softmax(top_v, axis=-1)
    gates = jnp.sum(jax.nn.one_hot(top_i, N_EXPERTS, dtype=jnp.float32) * top_w[..., None], axis=1)
    gates = gates.astype(x.dtype)
    y = jnp.zeros_like(xt)
    for e in range(N_EXPERTS):
        y = y + gates[:, e:e + 1] * _swiglu(xt, wg[e], wu[e], wd[e])
    return y.reshape(shp)


def _ffn(x, l, ffn_w_gate, ffn_w_up, ffn_w_down, router_w, router_b,
         exp_w_gate, exp_w_up, exp_w_down):
    i = l // 2
    if l % 2 == 0:
        return _swiglu(x, ffn_w_gate[i], ffn_w_up[i], ffn_w_down[i])
    return _moe(x, router_w[i], router_b[i], exp_w_gate[i], exp_w_up[i], exp_w_down[i])


def setup_inputs(seed: int = 0) -> dict:
    key = jax.random.key(seed)
    ks = jax.random.split(key, 32)
    f32 = jnp.float32

    def nrm(k, shape, scale):
        return jax.random.normal(k, shape, f32) * scale

    win_buf = min(WINDOW, PAST_LEN)
    return {
        "x_prompt": nrm(ks[0], (BATCH, SEQ, D_MODEL), 1.0),
        "x_sample": nrm(ks[1], (DEC_BATCH, DEC_SEQ, D_MODEL), 1.0),
        "mem_prompt": nrm(ks[2], (BATCH, N_MEM, D_MODEL), 1.0),
        "cache_win_k": nrm(ks[3], (DEPTH, DEC_BATCH, win_buf, N_KV_HEADS, HEAD_DIM), 1.0),
        "cache_win_v": nrm(ks[4], (DEPTH, DEC_BATCH, win_buf, N_KV_HEADS, HEAD_DIM), 1.0),
        "cache_conv": nrm(ks[5], (DEPTH, DEC_BATCH, CONV_WIDTH - 1, CONV_DIM), 1.0),
        "cache_mem_k": nrm(ks[6], (DEPTH, DEC_BATCH, N_MEM, MEM_HEADS, HEAD_DIM), 1.0),
        "cache_mem_v": nrm(ks[7], (DEPTH, DEC_BATCH, N_MEM, MEM_HEADS, HEAD_DIM), 1.0),
        "w_in": nrm(ks[8], (DEPTH, D_MODEL, IN_DIM), D_MODEL ** -0.5),
        "conv_w": nrm(ks[9], (DEPTH, CONV_WIDTH, CONV_DIM), CONV_WIDTH ** -0.5),
        "sink": nrm(ks[10], (DEPTH, N_HEADS), 1.0),
        "p_attn": nrm(ks[11], (DEPTH, ATTN_DIM, D_MODEL), ATTN_DIM ** -0.5),
        "p_conv": nrm(ks[12], (DEPTH, CONV_DIM, D_MODEL), CONV_DIM ** -0.5),
        "p_mem": nrm(ks[13], (DEPTH, MEM_DIM, D_MODEL), MEM_DIM ** -0.5),
        "w_o": nrm(ks[14], (DEPTH, D_MODEL, D_MODEL), BETA * D_MODEL ** -0.5),
        "w_mem_k": nrm(ks[15], (DEPTH, D_MODEL, MEM_DIM), D_MODEL ** -0.5),
        "w_mem_v": nrm(ks[16], (DEPTH, D_MODEL, MEM_DIM), D_MODEL ** -0.5),
        "ln1_g": 1.0 + nrm(ks[17], (DEPTH, D_MODEL), 0.02),
        "ln1_b": nrm(ks[18], (DEPTH, D_MODEL), 0.02),
        "ln2_g": 1.0 + nrm(ks[19], (DEPTH, D_MODEL), 0.02),
        "ln2_b": nrm(ks[20], (DEPTH, D_MODEL), 0.02),
        "ffn_w_gate": nrm(ks[21], (N_DENSE, D_MODEL, D_FF), D_MODEL ** -0.5),
        "ffn_w_up": nrm(ks[22], (N_DENSE, D_MODEL, D_FF), D_MODEL ** -0.5),
        "ffn_w_down": nrm(ks[23], (N_DENSE, D_FF, D_MODEL), BETA * D_FF ** -0.5),
        "router_w": nrm(ks[24], (N_MOE, D_MODEL, N_EXPERTS), D_MODEL ** -0.5),
        "router_b": nrm(ks[25], (N_MOE, N_EXPERTS), 0.01),
        "exp_w_gate": nrm(ks[26], (N_MOE, N_EXPERTS, D_MODEL, D_FF), D_MODEL ** -0.5),
        "exp_w_up": nrm(ks[27], (N_MOE, N_EXPERTS, D_MODEL, D_FF), D_MODEL ** -0.5),
        "exp_w_down": nrm(ks[28], (N_MOE, N_EXPERTS, D_FF, D_MODEL), BETA * D_FF ** -0.5),
    }


def reference(x_prompt, x_sample, mem_prompt, cache_win_k, cache_win_v, cache_conv,
              cache_mem_k, cache_mem_v, w_in, conv_w, sink, p_attn, p_conv, p_mem, w_o,
              w_mem_k, w_mem_v, ln1_g, ln1_b, ln2_g, ln2_b, ffn_w_gate, ffn_w_up,
              ffn_w_down, router_w, router_b, exp_w_gate, exp_w_up, exp_w_down):
    xp, xs = x_prompt, x_sample
    wk_p, wv_p, cv_p, mk_p, mv_p, wk_s, wv_s, cv_s = [], [], [], [], [], [], [], []
    b, n_mem = mem_prompt.shape[:2]
    for l in range(DEPTH):
        mk = (mem_prompt @ w_mem_k[l]).reshape(b, n_mem, MEM_HEADS, HEAD_DIM)
        mv = (mem_prompt @ w_mem_v[l]).reshape(b, n_mem, MEM_HEADS, HEAD_DIM)
        mix, nk, nv, nc = _mixer(xp, w_in[l], conv_w[l], sink[l], p_attn[l], p_conv[l],
                                 p_mem[l], w_o[l], mk, mv, None, None, None)
        xp = _layer_norm(ALPHA * xp + mix, ln1_g[l], ln1_b[l])
        xp = _layer_norm(ALPHA * xp + _ffn(xp, l, ffn_w_gate, ffn_w_up, ffn_w_down, router_w,
                                           router_b, exp_w_gate, exp_w_up, exp_w_down),
                         ln2_g[l], ln2_b[l])
        wk_p.append(nk); wv_p.append(nv); cv_p.append(nc); mk_p.append(mk); mv_p.append(mv)
        mix, nk, nv, nc = _mixer(xs, w_in[l], conv_w[l], sink[l], p_attn[l], p_conv[l],
                                 p_mem[l], w_o[l], cache_mem_k[l], cache_mem_v[l],
                                 cache_win_k[l], cache_win_v[l], cache_conv[l])
        xs = _layer_norm(ALPHA * xs + mix, ln1_g[l], ln1_b[l])
        xs = _layer_norm(ALPHA * xs + _ffn(xs, l, ffn_w_gate, ffn_w_up, ffn_w_down, router_w,
                                           router_b, exp_w_gate, exp_w_up, exp_w_down),
                         ln2_g[l], ln2_b[l])
        wk_s.append(nk); wv_s.append(nv); cv_s.append(nc)
    return (xp, xs, jnp.stack(wk_p), jnp.stack(wv_p), jnp.stack(cv_p), jnp.stack(mk_p),
            jnp.stack(mv_p), jnp.stack(wk_s), jnp.stack(wv_s), jnp.stack(cv_s))
```

```python
import functools

import jax
import jax.numpy as jnp
from jax import lax
from jax.experimental import pallas as pl
from jax.experimental.pallas import tpu as pltpu

F32 = jnp.float32
BF16 = jnp.bfloat16

D_MODEL = 1024
HEAD_DIM = 64
N_HEADS = 8
N_KV_HEADS = 2
GROUP = N_HEADS // N_KV_HEADS
ATTN_DIM = N_HEADS * HEAD_DIM
KV_DIM = N_KV_HEADS * HEAD_DIM
WINDOW = 128
ATTN_SCALE = HEAD_DIM ** -0.5
CONV_DIM = 256
CONV_WIDTH = 3
N_MEM = 256
MEM_HEADS = 4
MEM_DIM = MEM_HEADS * HEAD_DIM
D_FF = 2816
N_EXPERTS = 8
LN_EPS = 1e-5

_Q0, _K0, _V0, _H0, _GB0, _GC0, _QM0, _GA0 = 0, 512, 640, 768, 1024, 1280, 1536, 1792
_MIX_COLS = _GA0
IN_DIM = _GA0 + 3 * D_MODEL

LANES = 128
VMEM_LIMIT = 56 * 1024 * 1024

NEG_INF = float("-inf")


def _head_perm():
    cols = []
    for j in range(GROUP):
        cols += list(range(j * HEAD_DIM, (j + 1) * HEAD_DIM))
        cols += list(range((GROUP + j) * HEAD_DIM, (GROUP + j + 1) * HEAD_DIM))
    return jnp.array(cols, dtype=jnp.int32)


def _stack_head(s):
    return (s // 2) if s % 2 == 0 else GROUP + s // 2


def _layer_norm(v, g, b):
    mu = jnp.mean(v, axis=-1, keepdims=True)
    c = v - mu
    var = jnp.mean(c * c, axis=-1, keepdims=True)
    return c * lax.rsqrt(var + LN_EPS) * g + b


def _dot(a, b):
    return jnp.dot(a, b, preferred_element_type=F32)


def _dot_nt(a, b):
    return lax.dot_general(a, b, (((1,), (1,)), ((), ())), preferred_element_type=F32)


def _stack_lane_halves(q, n_blocks):
    m = q.shape[0]
    lo = lax.broadcasted_iota(jnp.int32, (m, LANES), 1) < HEAD_DIM
    parts = []
    for j in range(n_blocks):
        blk = q[:, j * LANES:(j + 1) * LANES]
        parts.append(jnp.where(lo, blk, 0.0))
        parts.append(jnp.where(lo, 0.0, blk))
    return jnp.concatenate(parts, axis=0)


def _unstack_lane_halves(r, m, n_blocks):
    lo = lax.broadcasted_iota(jnp.int32, (m, LANES), 1) < HEAD_DIM
    cols = [jnp.where(lo, r[(2 * j) * m:(2 * j + 1) * m], r[(2 * j + 1) * m:(2 * j + 2) * m])
            for j in range(n_blocks)]
    return jnp.concatenate(cols, axis=1)


def _merge_project_norm(x, zg, o_a, o_c, o_m, p_attn_ref, p_conv_ref, p_mem_ref, w_o_ref, g_ref, b_ref, alpha):
    merged = (jax.nn.sigmoid(zg[:, 0:D_MODEL]) * _dot(o_a.astype(BF16), p_attn_ref[...])
              + jax.nn.sigmoid(zg[:, D_MODEL:2 * D_MODEL]) * _dot(o_c.astype(BF16), p_conv_ref[...])
              + jax.nn.sigmoid(zg[:, 2 * D_MODEL:3 * D_MODEL]) * _dot(o_m.astype(BF16), p_mem_ref[...]))
    mix = _dot(merged.astype(BF16), w_o_ref[...])
    return _layer_norm(alpha * x + mix, g_ref[...], b_ref[...])


def _mixer_prompt_kernel(sink_ref, x_ref, mem_ref, w_in_ref, convw_ref, p_attn_ref, p_conv_ref, p_mem_ref,
                         w_o_ref, wmk_ref, wmv_ref, g_ref, b_ref,
                         x1_ref, wk_ref, wv_ref, cv_ref, mk_ref, mv_ref,
                         kprev, vprev, uprev, mk_s, mv_s, *, ts, alpha):
    s = pl.program_id(1)
    ns = pl.num_programs(1)
    x = x_ref[0]
    xb = x.astype(BF16)

    @pl.when(s == 0)
    def _():
        memb = mem_ref[0].astype(BF16)
        mk = _dot(memb, wmk_ref[...])
        mv = _dot(memb, wmv_ref[...])
        mk_ref[0] = mk
        mv_ref[0] = mv
        mk_s[...] = mk.astype(BF16)
        mv_s[...] = mv.astype(BF16)
        kprev[...] = jnp.zeros_like(kprev)
        vprev[...] = jnp.zeros_like(vprev)
        uprev[...] = jnp.zeros_like(uprev)

    z = _dot(xb, w_in_ref[:, 0:_MIX_COLS])
    q = z[:, _Q0:_K0] * ATTN_SCALE
    k = z[:, _K0:_V0]
    v = z[:, _V0:_H0]
    hc = z[:, _H0:_GB0]
    g_b = z[:, _GB0:_GC0]
    g_c = z[:, _GC0:_QM0]
    qm = z[:, _QM0:_GA0] * ATTN_SCALE

    k_ext = jnp.concatenate([kprev[...], k.astype(BF16)], axis=0)
    v_ext = jnp.concatenate([vprev[...], v.astype(BF16)], axis=0)
    row_i = lax.broadcasted_iota(jnp.int32, (WINDOW, 2 * WINDOW), 0)
    col_j = lax.broadcasted_iota(jnp.int32, (WINDOW, 2 * WINDOW), 1)
    band = (col_j > row_i) & (col_j <= row_i + WINDOW)
    band_first = band & ((col_j >= WINDOW) | (s > 0))
    o_blocks = []
    for blk in range(ts // WINDOW):
        qs = _stack_lane_halves(q[blk * WINDOW:(blk + 1) * WINDOW], GROUP).astype(BF16)
        kband = k_ext[blk * WINDOW:blk * WINDOW + 2 * WINDOW]
        vband = v_ext[blk * WINDOW:blk * WINDOW + 2 * WINDOW]
        sc = _dot_nt(qs, kband)
        mask = band_first if blk == 0 else band
        ps = []
        for hh in range(N_HEADS):
            sink_h = sink_ref[_stack_head(hh)]
            sch = jnp.where(mask, sc[hh * WINDOW:(hh + 1) * WINDOW], NEG_INF)
            m = jnp.maximum(jnp.max(sch, axis=-1, keepdims=True), sink_h)
            e = jnp.exp(sch - m)
            den = jnp.sum(e, axis=-1, keepdims=True) + jnp.exp(sink_h - m)
            ps.append((e / den).astype(BF16))
        r = _dot(jnp.concatenate(ps, axis=0), vband)
        o_blocks.append(_unstack_lane_halves(r, WINDOW, GROUP))
    o_a = jnp.concatenate(o_blocks, axis=0)
    kprev[...] = k_ext[ts:ts + WINDOW]
    vprev[...] = v_ext[ts:ts + WINDOW]

    u = g_c * hc
    rows = lax.broadcasted_iota(jnp.int32, (ts, CONV_DIM), 0)
    up1 = uprev[7:8, :]
    up2 = uprev[6:7, :]
    u_m1 = jnp.where(rows == 0, up1, pltpu.roll(u, 1, 0))
    u_m2 = jnp.where(rows == 0, up2, jnp.where(rows == 1, up1, pltpu.roll(u, 2, 0)))
    cw = convw_ref[...]
    conv = cw[0:1, :] * u_m2 + cw[1:2, :] * u_m1 + cw[2:3, :] * u
    o_c = g_b * conv
    uprev[...] = u[ts - 8:ts]

    qms = _stack_lane_halves(qm, MEM_HEADS // 2).astype(BF16)
    o_cols = []
    lo = lax.broadcasted_iota(jnp.int32, (ts, LANES), 1) < HEAD_DIM
    for jb in range(MEM_HEADS // 2):
        scm = _dot_nt(qms[2 * jb * ts:(2 * jb + 2) * ts], mk_s[:, jb * LANES:(jb + 1) * LANES])
        mm = jnp.max(scm, axis=-1, keepdims=True)
        em = jnp.exp(scm - mm)
        pm = (em / jnp.sum(em, axis=-1, keepdims=True)).astype(BF16)
        rm = _dot(pm, mv_s[:, jb * LANES:(jb + 1) * LANES])
        o_cols.append(jnp.where(lo, rm[0:ts], rm[ts:2 * ts]))
    o_m = jnp.concatenate(o_cols, axis=1)

    zg = _dot(xb, w_in_ref[:, _GA0:IN_DIM])
    x1_ref[0] = _merge_project_norm(x, zg, o_a, o_c, o_m, p_attn_ref, p_conv_ref, p_mem_ref, w_o_ref,
                                    g_ref, b_ref, alpha)

    @pl.when(s == ns - 1)
    def _():
        wk_ref[0] = k[ts - WINDOW:ts]
        wv_ref[0] = v[ts - WINDOW:ts]
        cv_ref[0] = u[ts - (CONV_WIDTH - 1):ts]


def _const_spec(shape):
    n = len(shape)
    return pl.BlockSpec(shape, lambda *_: (0,) * n, pipeline_mode=pl.Buffered(1))


def _mixer_prompt(x, mem, w, l, *, ts, alpha):
    b, seq, _ = x.shape
    kern = functools.partial(_mixer_prompt_kernel, ts=ts, alpha=alpha)
    smem = pl.BlockSpec(memory_space=pltpu.SMEM)
    in_specs = [
        smem,
        pl.BlockSpec((1, ts, D_MODEL), lambda i, s: (i, s, 0)),
        pl.BlockSpec((1, N_MEM, D_MODEL), lambda i, s: (i, 0, 0)),
        _const_spec((D_MODEL, IN_DIM)),
        _const_spec((CONV_WIDTH, CONV_DIM)),
        _const_spec((ATTN_DIM, D_MODEL)),
        _const_spec((CONV_DIM, D_MODEL)),
        _const_spec((MEM_DIM, D_MODEL)),
        _const_spec((D_MODEL, D_MODEL)),
        _const_spec((D_MODEL, MEM_DIM)),
        _const_spec((D_MODEL, MEM_DIM)),
        _const_spec((1, D_MODEL)),
        _const_spec((1, D_MODEL)),
    ]
    out_shape = (
        jax.ShapeDtypeStruct((b, seq, D_MODEL), F32),
        jax.ShapeDtypeStruct((b, WINDOW, KV_DIM), F32),
        jax.ShapeDtypeStruct((b, WINDOW, KV_DIM), F32),
        jax.ShapeDtypeStruct((b, CONV_WIDTH - 1, CONV_DIM), F32),
        jax.ShapeDtypeStruct((b, N_MEM, MEM_DIM), F32),
        jax.ShapeDtypeStruct((b, N_MEM, MEM_DIM), F32),
    )
    out_specs = (
        pl.BlockSpec((1, ts, D_MODEL), lambda i, s: (i, s, 0)),
        pl.BlockSpec((1, WINDOW, KV_DIM), lambda i, s: (i, 0, 0)),
        pl.BlockSpec((1, WINDOW, KV_DIM), lambda i, s: (i, 0, 0)),
        pl.BlockSpec((1, CONV_WIDTH - 1, CONV_DIM), lambda i, s: (i, 0, 0)),
        pl.BlockSpec((1, N_MEM, MEM_DIM), lambda i, s: (i, 0, 0)),
        pl.BlockSpec((1, N_MEM, MEM_DIM), lambda i, s: (i, 0, 0)),
    )
    scratch = [
        pltpu.VMEM((WINDOW, KV_DIM), BF16),
        pltpu.VMEM((WINDOW, KV_DIM), BF16),
        pltpu.VMEM((8, CONV_DIM), F32),
        pltpu.VMEM((N_MEM, MEM_DIM), BF16),
        pltpu.VMEM((N_MEM, MEM_DIM), BF16),
    ]
    return pl.pallas_call(
        kern,
        grid=(b, seq // ts),
        in_specs=in_specs,
        out_specs=out_specs,
        out_shape=out_shape,
        scratch_shapes=scratch,
        compiler_params=pltpu.CompilerParams(dimension_semantics=("arbitrary", "arbitrary"),
                                             vmem_limit_bytes=VMEM_LIMIT),
        name=f"mixer_prompt_l{l}",
    )(w["sink"], x, mem, w["w_in"], w["conv_w"], w["p_attn"], w["p_conv"], w["p_mem"], w["w_o"],
      w["w_mem_k"], w["w_mem_v"], w["ln1_g"], w["ln1_b"])


def _mixer_sample_kernel(sink_ref, x_ref, ck_ref, cvv_ref, cc_ref, cmk_ref, cmv_ref, w_in_ref, convw_ref,
                         p_attn_ref, p_conv_ref, p_mem_ref, w_o_ref, g_ref, b_ref,
                         x1_ref, wk_ref, wv_ref, cv_ref, *, bb, t, alpha):
    nbuf = ck_ref.shape[1]
    x = x_ref[...]
    xb = x.astype(BF16)
    z = _dot(xb, w_in_ref[:, 0:_MIX_COLS])
    q = z[:, _Q0:_K0] * ATTN_SCALE
    k = z[:, _K0:_V0]
    v = z[:, _V0:_H0]
    u = z[:, _GC0:_QM0] * z[:, _H0:_GB0]
    g_b = z[:, _GB0:_GC0]
    qm = z[:, _QM0:_GA0] * ATTN_SCALE
    cw = convw_ref[...]

    kpad = 2 * WINDOW - nbuf - t
    nstk = N_HEADS * t
    r_t = lax.broadcasted_iota(jnp.int32, (nstk, 2 * WINDOW), 0) % t
    c_j = lax.broadcasted_iota(jnp.int32, (nstk, 2 * WINDOW), 1)
    dist = r_t + nbuf - c_j
    wmask = (dist >= 0) & (dist < WINDOW) & (c_j < nbuf + t)
    blk_id = lax.broadcasted_iota(jnp.int32, (nstk, 1), 0) // t
    sink_col = jnp.zeros((nstk, 1), F32)
    for hh in range(N_HEADS):
        sink_col = jnp.where(blk_id == hh, sink_ref[_stack_head(hh)], sink_col)
    mlane = lax.broadcasted_iota(jnp.int32, (t, MEM_DIM), 1) // HEAD_DIM
    zpad_k = jnp.zeros((kpad, KV_DIM), F32)

    o_a_rows, o_c_rows, o_m_rows = [], [], []
    for bi in range(bb):
        sl = slice(bi * t, (bi + 1) * t)
        kc = ck_ref[bi]
        vc = cvv_ref[bi]
        k_all = jnp.concatenate([kc, k[sl], zpad_k], axis=0).astype(BF16)
        v_all = jnp.concatenate([vc, v[sl], zpad_k], axis=0).astype(BF16)
        qs = _stack_lane_halves(q[sl], GROUP).astype(BF16)
        sc = jnp.where(wmask, _dot_nt(qs, k_all), NEG_INF)
        m = jnp.maximum(jnp.max(sc, axis=-1, keepdims=True), sink_col)
        e = jnp.exp(sc - m)
        den = jnp.sum(e, axis=-1, keepdims=True) + jnp.exp(sink_col - m)
        r = _dot((e / den).astype(BF16), v_all)
        o_a_rows.append(_unstack_lane_halves(r, t, GROUP))
        wk_ref[bi] = jnp.concatenate([kc[t:nbuf], k[sl]], axis=0)
        wv_ref[bi] = jnp.concatenate([vc[t:nbuf], v[sl]], axis=0)
        u_ext = jnp.concatenate([cc_ref[bi], u[sl]], axis=0)
        conv = cw[0:1, :] * u_ext[0:t] + cw[1:2, :] * u_ext[1:t + 1] + cw[2:3, :] * u_ext[2:t + 2]
        o_c_rows.append(g_b[sl] * conv)
        cv_ref[bi] = u_ext[t:t + CONV_WIDTH - 1]
        qmb = qm[sl]
        qms = jnp.concatenate([jnp.where(mlane == hh, qmb, 0.0) for hh in range(MEM_HEADS)], axis=0).astype(BF16)
        scm = _dot_nt(qms, cmk_ref[bi].astype(BF16))
        mm = jnp.max(scm, axis=-1, keepdims=True)
        em = jnp.exp(scm - mm)
        pm = (em / jnp.sum(em, axis=-1, keepdims=True)).astype(BF16)
        rm = _dot(pm, cmv_ref[bi].astype(BF16))
        om = jnp.zeros((t, MEM_DIM), F32)
        for hh in range(MEM_HEADS):
            om = jnp.where(mlane == hh, rm[hh * t:(hh + 1) * t], om)
        o_m_rows.append(om)

    o_a = jnp.concatenate(o_a_rows, axis=0)
    o_c = jnp.concatenate(o_c_rows, axis=0)
    o_m = jnp.concatenate(o_m_rows, axis=0)
    zg = _dot(xb, w_in_ref[:, _GA0:IN_DIM])
    x1_ref[...] = _merge_project_norm(x, zg, o_a, o_c, o_m, p_attn_ref, p_conv_ref, p_mem_ref, w_o_ref,
                                      g_ref, b_ref, alpha)


def _mixer_sample(x2d, ck, cv, cc, cmk, cmv, w, l, *, bb, t, alpha):
    nb = ck.shape[0]
    nbuf = ck.shape[1]
    assert nbuf + t <= 2 * WINDOW and nbuf % 8 == 0 and t % 8 == 0
    kern = functools.partial(_mixer_sample_kernel, bb=bb, t=t, alpha=alpha)
    smem = pl.BlockSpec(memory_space=pltpu.SMEM)
    in_specs = [
        smem,
        pl.BlockSpec((bb * t, D_MODEL), lambda i: (i, 0)),
        pl.BlockSpec((bb, nbuf, KV_DIM), lambda i: (i, 0, 0)),
        pl.BlockSpec((bb, nbuf, KV_DIM), lambda i: (i, 0, 0)),
        pl.BlockSpec((bb, CONV_WIDTH - 1, CONV_DIM), lambda i: (i, 0, 0)),
        pl.BlockSpec((bb, N_MEM, MEM_DIM), lambda i: (i, 0, 0)),
        pl.BlockSpec((bb, N_MEM, MEM_DIM), lambda i: (i, 0, 0)),
        _const_spec((D_MODEL, IN_DIM)),
        _const_spec((CONV_WIDTH, CONV_DIM)),
        _const_spec((ATTN_DIM, D_MODEL)),
        _const_spec((CONV_DIM, D_MODEL)),
        _const_spec((MEM_DIM, D_MODEL)),
        _const_spec((D_MODEL, D_MODEL)),
        _const_spec((1, D_MODEL)),
        _const_spec((1, D_MODEL)),
    ]
    out_shape = (
        jax.ShapeDtypeStruct((nb * t, D_MODEL), F32),
        jax.ShapeDtypeStruct((nb, nbuf, KV_DIM), F32),
        jax.ShapeDtypeStruct((nb, nbuf, KV_DIM), F32),
        jax.ShapeDtypeStruct((nb, CONV_WIDTH - 1, CONV_DIM), F32),
    )
    out_specs = (
        pl.BlockSpec((bb * t, D_MODEL), lambda i: (i, 0)),
        pl.BlockSpec((bb, nbuf, KV_DIM), lambda i: (i, 0, 0)),
        pl.BlockSpec((bb, nbuf, KV_DIM), lambda i: (i, 0, 0)),
        pl.BlockSpec((bb, CONV_WIDTH - 1, CONV_DIM), lambda i: (i, 0, 0)),
    )
    return pl.pallas_call(
        kern,
        grid=(nb // bb,),
        in_specs=in_specs,
        out_specs=out_specs,
        out_shape=out_shape,
        compiler_params=pltpu.CompilerParams(dimension_semantics=("arbitrary",),
                                             vmem_limit_bytes=VMEM_LIMIT),
        name=f"mixer_sample_l{l}",
    )(w["sink"], x2d, ck, cv, cc, cmk, cmv, w["w_in"], w["conv_w"], w["p_attn"], w["p_conv"], w["p_mem"],
      w["w_o"], w["ln1_g"], w["ln1_b"])


def _swiglu(xb, wg, wu, wd):
    g = _dot(xb, wg)
    u = _dot(xb, wu)
    return _dot((jax.nn.silu(g) * u).astype(BF16), wd)


def _ffn_kernel(x_ref, wg_ref, wu_ref, wd_ref, g_ref, b_ref, o_ref, *, alpha):
    x = x_ref[...]
    y = _swiglu(x.astype(BF16), wg_ref[...], wu_ref[...], wd_ref[...])
    o_ref[...] = _layer_norm(alpha * x + y, g_ref[...], b_ref[...])


def _ffn_dense(x, wg, wu, wd, g, b, *, tm, alpha, name):
    n = x.shape[0]
    return pl.pallas_call(
        functools.partial(_ffn_kernel, alpha=alpha),
        grid=(n // tm,),
        in_specs=[pl.BlockSpec((tm, D_MODEL), lambda i: (i, 0)),
                  _const_spec((D_MODEL, D_FF)), _const_spec((D_MODEL, D_FF)), _const_spec((D_FF, D_MODEL)),
                  _const_spec((1, D_MODEL)), _const_spec((1, D_MODEL))],
        out_specs=pl.BlockSpec((tm, D_MODEL), lambda i: (i, 0)),
        out_shape=jax.ShapeDtypeStruct((n, D_MODEL), F32),
        compiler_params=pltpu.CompilerParams(dimension_semantics=("arbitrary",), vmem_limit_bytes=VMEM_LIMIT),
        name=name,
    )(x, wg, wu, wd, g, b)


def _route_kernel(x_ref, wrt_ref, rb_ref, tri_ref, slab_ref, cnt_ref, carry, *, tm):
    i = pl.program_id(0)

    @pl.when(i == 0)
    def _():
        carry[...] = jnp.zeros_like(carry)

    lg = _dot_nt(wrt_ref[...], x_ref[...].astype(BF16)) + rb_ref[...]
    e_iota = lax.broadcasted_iota(jnp.int32, (N_EXPERTS, tm), 0)
    m1 = jnp.max(lg, axis=0, keepdims=True)
    i1 = jnp.min(jnp.where(lg == m1, e_iota, N_EXPERTS), axis=0, keepdims=True)
    lg2 = jnp.where(e_iota == i1, NEG_INF, lg)
    m2 = jnp.max(lg2, axis=0, keepdims=True)
    i2 = jnp.min(jnp.where(lg2 == m2, e_iota, N_EXPERTS), axis=0, keepdims=True)
    e2 = jnp.exp(m2 - m1)
    den = 1.0 + e2
    w1 = 1.0 / den
    w2 = e2 / den
    sel1 = e_iota == i1
    sel2 = e_iota == i2
    oh = jnp.where(sel1 | sel2, 1.0, 0.0)
    pre = _dot(oh.astype(BF16), tri_ref[...]) + carry[:, 0:1]
    r1 = jnp.sum(jnp.where(sel1, pre, 0.0), axis=0, keepdims=True)
    r2 = jnp.sum(jnp.where(sel2, pre, 0.0), axis=0, keepdims=True)
    carry[...] = carry[...] + jnp.sum(oh, axis=1, keepdims=True)
    rows = [i1.astype(F32), i2.astype(F32), r1, r2, w1, w2]
    slab = jnp.zeros((N_EXPERTS, tm), F32)
    for ri, rv in enumerate(rows):
        slab = jnp.where(e_iota == ri, rv, slab)
    slab_ref[...] = slab
    cnt_ref[...] = carry[...]


def _route(x, wrt, rb, *, tm):
    n = x.shape[0]
    tri = jnp.triu(jnp.ones((tm, tm), BF16), k=1)
    return pl.pallas_call(
        functools.partial(_route_kernel, tm=tm),
        grid=(n // tm,),
        in_specs=[pl.BlockSpec((tm, D_MODEL), lambda i: (i, 0)),
                  _const_spec((N_EXPERTS, D_MODEL)), _const_spec((N_EXPERTS, 1)), _const_spec((tm, tm))],
        out_specs=(pl.BlockSpec((N_EXPERTS, tm), lambda i: (0, i)),
                   pl.BlockSpec((N_EXPERTS, LANES), lambda i: (0, 0))),
        out_shape=(jax.ShapeDtypeStruct((N_EXPERTS, n), F32), jax.ShapeDtypeStruct((N_EXPERTS, LANES), F32)),
        scratch_shapes=[pltpu.VMEM((N_EXPERTS, LANES), F32)],
        compiler_params=pltpu.CompilerParams(dimension_semantics=("arbitrary",)),
        name="moe_route",
    )(x, wrt, rb, tri)


def _row_copy(src, src_row, dst, dst_row, sem):
    return pltpu.make_async_copy(src.at[pl.ds(src_row, 1)], dst.at[pl.ds(dst_row, 1)], sem)


def _dispatch_kernel(slot_ref, x_ref, init_ref, xs_ref, sem, *, tm):
    del init_ref

    def issue(r, c):
        _row_copy(x_ref, r, xs_ref, slot_ref[0, 0, 2 * r], sem).start()
        _row_copy(x_ref, r, xs_ref, slot_ref[0, 0, 2 * r + 1], sem).start()
        return c

    lax.fori_loop(0, tm, issue, 0, unroll=8)

    def drain(r, c):
        _row_copy(x_ref, 0, xs_ref, 0, sem).wait()
        _row_copy(x_ref, 0, xs_ref, 0, sem).wait()
        return c

    lax.fori_loop(0, tm, drain, 0, unroll=8)


def _dispatch(x, slots3, n_rows, *, tm):
    n = x.shape[0]
    init = jnp.zeros((n_rows, D_MODEL), F32)
    return pl.pallas_call(
        functools.partial(_dispatch_kernel, tm=tm),
        grid=(n // tm,),
        in_specs=[pl.BlockSpec((1, 1, 2 * tm), lambda i: (i, 0, 0), memory_space=pltpu.SMEM),
                  pl.BlockSpec((tm, D_MODEL), lambda i: (i, 0)),
                  pl.BlockSpec(memory_space=pl.ANY)],
        out_specs=pl.BlockSpec(memory_space=pl.ANY),
        out_shape=jax.ShapeDtypeStruct((n_rows, D_MODEL), F32),
        scratch_shapes=[pltpu.SemaphoreType.DMA(())],
        input_output_aliases={2: 0},
        compiler_params=pltpu.CompilerParams(dimension_semantics=("arbitrary",)),
        name="moe_dispatch",
    )(slots3, x, init)


def _expert_kernel(te_ref, nt_ref, xs_ref, wg_ref, wu_ref, wd_ref, o_ref):
    del te_ref

    live = pl.program_id(0) < nt_ref[0]

    @pl.when(live)
    def _():
        o_ref[...] = _swiglu(xs_ref[...].astype(BF16), wg_ref[0], wu_ref[0], wd_ref[0])

    @pl.when(jnp.logical_not(live))
    def _():
        o_ref[...] = jnp.zeros_like(o_ref)


def _experts(xs, tile_expert, n_tiles, wg, wu, wd, *, tm):
    n_rows = xs.shape[0]

    def row_map(i, te, nt):
        return (jnp.minimum(i, nt[0] - 1), 0)

    def w_map(i, te, nt):
        return (te[i], 0, 0)

    return pl.pallas_call(
        _expert_kernel,
        grid_spec=pltpu.PrefetchScalarGridSpec(
            num_scalar_prefetch=2,
            grid=(n_rows // tm,),
            in_specs=[pl.BlockSpec((tm, D_MODEL), row_map),
                      pl.BlockSpec((1, D_MODEL, D_FF), w_map),
                      pl.BlockSpec((1, D_MODEL, D_FF), w_map),
                      pl.BlockSpec((1, D_FF, D_MODEL), w_map)],
            out_specs=pl.BlockSpec((tm, D_MODEL), lambda i, te, nt: (i, 0)),
        ),
        out_shape=jax.ShapeDtypeStruct((n_rows, D_MODEL), F32),
        compiler_params=pltpu.CompilerParams(dimension_semantics=("arbitrary",), vmem_limit_bytes=VMEM_LIMIT),
        name="moe_experts",
    )(tile_expert, n_tiles, xs, wg, wu, wd)


def _combine_kernel(slot_ref, x_ref, gate_ref, ys_ref, g_ref, b_ref, o_ref, buf0, buf1, sem, *, tm, alpha):
    def issue(r, c):
        _row_copy(ys_ref, slot_ref[0, 0, 2 * r], buf0, r, sem).start()
        _row_copy(ys_ref, slot_ref[0, 0, 2 * r + 1], buf1, r, sem).start()
        return c

    lax.fori_loop(0, tm, issue, 0, unroll=8)

    def drain(r, c):
        _row_copy(ys_ref, 0, buf0, 0, sem).wait()
        _row_copy(ys_ref, 0, buf1, 0, sem).wait()
        return c

    lax.fori_loop(0, tm, drain, 0, unroll=8)
    gates = gate_ref[...]
    y = gates[:, 0:1] * buf0[...] + gates[:, 1:2] * buf1[...]
    o_ref[...] = _layer_norm(alpha * x_ref[...] + y, g_ref[...], b_ref[...])


def _combine(x, slots3, gates, ys, g, b, *, tm, alpha):
    n = x.shape[0]
    return pl.pallas_call(
        functools.partial(_combine_kernel, tm=tm, alpha=alpha),
        grid=(n // tm,),
        in_specs=[pl.BlockSpec((1, 1, 2 * tm), lambda i: (i, 0, 0), memory_space=pltpu.SMEM),
                  pl.BlockSpec((tm, D_MODEL), lambda i: (i, 0)),
                  pl.BlockSpec((tm, 2), lambda i: (i, 0)),
                  pl.BlockSpec(memory_space=pl.ANY),
                  _const_spec((1, D_MODEL)), _const_spec((1, D_MODEL))],
        out_specs=pl.BlockSpec((tm, D_MODEL), lambda i: (i, 0)),
        out_shape=jax.ShapeDtypeStruct((n, D_MODEL), F32),
        scratch_shapes=[pltpu.VMEM((tm, D_MODEL), F32), pltpu.VMEM((tm, D_MODEL), F32),
                        pltpu.SemaphoreType.DMA(())],
        compiler_params=pltpu.CompilerParams(dimension_semantics=("arbitrary",)),
        name="moe_combine",
    )(slots3, x, gates, ys, g, b)


def _moe(x, router_w, router_b, wg, wu, wd, g, b, *, alpha, tm_route, tm_rows, tm_expert):
    n = x.shape[0]
    slab, cnt = _route(x, router_w.T.astype(BF16), router_b.reshape(N_EXPERTS, 1), tm=tm_route)
    counts = cnt[:, 0].astype(jnp.int32)
    tiles_per = (counts + tm_expert - 1) // tm_expert
    ends = jnp.cumsum(tiles_per)
    offs = (ends - tiles_per) * tm_expert
    i1 = slab[0].astype(jnp.int32)
    i2 = slab[1].astype(jnp.int32)
    eye = jnp.arange(N_EXPERTS, dtype=jnp.int32)[:, None]
    off1 = jnp.sum(jnp.where(eye == i1[None, :], offs[:, None], 0), axis=0)
    off2 = jnp.sum(jnp.where(eye == i2[None, :], offs[:, None], 0), axis=0)
    slots = jnp.stack([off1 + slab[2].astype(jnp.int32), off2 + slab[3].astype(jnp.int32)], axis=1)
    gates = jnp.stack([slab[4], slab[5]], axis=1)
    n_rows = 2 * n + N_EXPERTS * tm_expert
    max_tiles = n_rows // tm_expert
    tile_expert = jnp.minimum(
        jnp.sum(jnp.arange(max_tiles, dtype=jnp.int32)[:, None] >= ends[None, :], axis=1), N_EXPERTS - 1
    ).astype(jnp.int32)
    n_tiles = ends[-1:].astype(jnp.int32)
    slots3 = slots.reshape(n // tm_rows, 1, 2 * tm_rows)

    xs = _dispatch(x, slots3, n_rows, tm=tm_rows)
    ys = _experts(xs, tile_expert, n_tiles, wg, wu, wd, tm=tm_expert)
    return _combine(x, slots3, gates, ys, g, b, tm=tm_rows, alpha=alpha)


def _layer_weights(l, w_in, conv_w, sink, p_attn, p_conv, p_mem, w_o, w_mem_k, w_mem_v, ln1_g, ln1_b):
    perm = _head_perm()
    w_in_l = w_in[l]
    w_in_l = jnp.concatenate([w_in_l[:, :ATTN_DIM][:, perm], w_in_l[:, ATTN_DIM:]], axis=1)
    return dict(
        sink=sink[l],
        w_in=w_in_l.astype(BF16),
        conv_w=conv_w[l],
        p_attn=p_attn[l][perm, :].astype(BF16),
        p_conv=p_conv[l].astype(BF16),
        p_mem=p_mem[l].astype(BF16),
        w_o=w_o[l].astype(BF16),
        w_mem_k=w_mem_k[l].astype(BF16),
        w_mem_v=w_mem_v[l].astype(BF16),
        ln1_g=ln1_g[l].reshape(1, D_MODEL),
        ln1_b=ln1_b[l].reshape(1, D_MODEL),
    )


def kernel(x_prompt, x_sample, mem_prompt, cache_win_k, cache_win_v, cache_conv, cache_mem_k, cache_mem_v, w_in, conv_w, sink, p_attn, p_conv, p_mem, w_o, w_mem_k, w_mem_v, ln1_g, ln1_b, ln2_g, ln2_b, ffn_w_gate, ffn_w_up, ffn_w_down, router_w, router_b, exp_w_gate, exp_w_up, exp_w_down):
    depth = w_in.shape[0]
    alpha = (2 * depth) ** 0.25
    b, seq, _ = x_prompt.shape
    nb, t, _ = x_sample.shape
    nbuf = cache_win_k.shape[2]
    n_p = b * seq
    n_s = nb * t

    xp = x_prompt
    xs = x_sample.reshape(n_s, D_MODEL)
    ck = cache_win_k.reshape(depth, nb, nbuf, KV_DIM)
    cv = cache_win_v.reshape(depth, nb, nbuf, KV_DIM)
    cmk = cache_mem_k.reshape(depth, nb, N_MEM, MEM_DIM)
    cmv = cache_mem_v.reshape(depth, nb, N_MEM, MEM_DIM)

    outs = {k_: [] for k_ in ("wk_p", "wv_p", "cv_p", "mk_p", "mv_p", "wk_s", "wv_s", "cv_s")}
    for l in range(depth):
        w = _layer_weights(l, w_in, conv_w, sink, p_attn, p_conv, p_mem, w_o, w_mem_k, w_mem_v, ln1_g, ln1_b)
        x1p, wk, wv, cvp, mk, mv = _mixer_prompt(xp, mem_prompt, w, l, ts=512, alpha=alpha)
        x1s, wks, wvs, cvs = _mixer_sample(xs, ck[l], cv[l], cache_conv[l], cmk[l], cmv[l], w, l,
                                           bb=16, t=t, alpha=alpha)
        outs["wk_p"].append(wk); outs["wv_p"].append(wv); outs["cv_p"].append(cvp)
        outs["mk_p"].append(mk); outs["mv_p"].append(mv)
        outs["wk_s"].append(wks); outs["wv_s"].append(wvs); outs["cv_s"].append(cvs)
        x1 = jnp.concatenate([x1p.reshape(n_p, D_MODEL), x1s], axis=0)
        g2 = ln2_g[l].reshape(1, D_MODEL)
        b2 = ln2_b[l].reshape(1, D_MODEL)
        i = l // 2
        if l % 2 == 0:
            x2 = _ffn_dense(x1, ffn_w_gate[i].astype(BF16), ffn_w_up[i].astype(BF16), ffn_w_down[i].astype(BF16),
                            g2, b2, tm=512, alpha=alpha, name=f"ffn_dense_l{l}")
        else:
            x2 = _moe(x1, router_w[i], router_b[i], exp_w_gate[i].astype(BF16), exp_w_up[i].astype(BF16),
                      exp_w_down[i].astype(BF16), g2, b2, alpha=alpha, tm_route=512, tm_rows=512, tm_expert=256)
        xp = x2[:n_p].reshape(b, seq, D_MODEL)
        xs = x2[n_p:]

    def stk(name, shape):
        return jnp.stack(outs[name]).reshape(shape)

    return (xp, xs.reshape(nb, t, D_MODEL),
            stk("wk_p", (depth, b, WINDOW, N_KV_HEADS, HEAD_DIM)),
            stk("wv_p", (depth, b, WINDOW, N_KV_HEADS, HEAD_DIM)),
            stk("cv_p", (depth, b, CONV_WIDTH - 1, CONV_DIM)),
            stk("mk_p", (depth, b, N_MEM, MEM_HEADS, HEAD_DIM)),
            stk("mv_p", (depth, b, N_MEM, MEM_HEADS, HEAD_DIM)),
            stk("wk_s", (depth, nb, nbuf, N_KV_HEADS, HEAD_DIM)),
            stk("wv_s", (depth, nb, nbuf, N_KV_HEADS, HEAD_DIM)),
            stk("cv_s", (depth, nb, CONV_WIDTH - 1, CONV_DIM)))
```

```python
import functools

import jax
import jax.numpy as jnp
from jax import lax
from jax.experimental import pallas as pl
from jax.experimental.pallas import tpu as pltpu

F32 = jnp.float32
BF16 = jnp.bfloat16

D_MODEL = 1024
HEAD_DIM = 64
N_HEADS = 8
N_KV_HEADS = 2
GROUP = N_HEADS // N_KV_HEADS
ATTN_DIM = N_HEADS * HEAD_DIM
KV_DIM = N_KV_HEADS * HEAD_DIM
WINDOW = 128
ATTN_SCALE = HEAD_DIM ** -0.5
CONV_DIM = 256
CONV_WIDTH = 3
N_MEM = 256
MEM_HEADS = 4
MEM_DIM = MEM_HEADS * HEAD_DIM
D_FF = 2816
N_EXPERTS = 8
LN_EPS = 1e-5

_Q0, _K0, _V0, _H0, _GB0, _GC0, _QM0, _GA0 = 0, 512, 640, 768, 1024, 1280, 1536, 1792
_MIX_COLS = _GA0
IN_DIM = _GA0 + 3 * D_MODEL

LANES = 128
VMEM_LIMIT = 56 * 1024 * 1024

NEG_INF = float("-inf")


def _stack_head(s):
    return (s // 2) if s % 2 == 0 else GROUP + s // 2


def _layer_norm(v, g, b):
    mu = jnp.mean(v, axis=-1, keepdims=True)
    c = v - mu
    var = jnp.mean(c * c, axis=-1, keepdims=True)
    return c * lax.rsqrt(var + LN_EPS) * g + b


def _dot(a, b):
    return jnp.dot(a, b, preferred_element_type=F32)


def _dot_nt(a, b):
    return lax.dot_general(a, b, (((1,), (1,)), ((), ())), preferred_element_type=F32)


def _stack_lane_halves(q, n_blocks):
    m = q.shape[0]
    lo = lax.broadcasted_iota(jnp.int32, (m, LANES), 1) < HEAD_DIM
    parts = []
    for j in range(n_blocks):
        blk = q[:, j * LANES:(j + 1) * LANES]
        parts.append(jnp.where(lo, blk, 0.0))
        parts.append(jnp.where(lo, 0.0, blk))
    return jnp.concatenate(parts, axis=0)


def _unstack_lane_halves(r, m, n_blocks):
    lo = lax.broadcasted_iota(jnp.int32, (m, LANES), 1) < HEAD_DIM
    cols = [jnp.where(lo, r[(2 * j) * m:(2 * j + 1) * m], r[(2 * j + 1) * m:(2 * j + 2) * m])
            for j in range(n_blocks)]
    return jnp.concatenate(cols, axis=1)


def _merge_project_norm(x, zg, o_a, o_c, o_m, p_attn_ref, p_conv_ref, p_mem_ref, w_o_ref, g_ref, b_ref, alpha):
    merged = (jax.nn.sigmoid(zg[:, 0:D_MODEL]) * _dot(o_a.astype(BF16), p_attn_ref[...])
              + jax.nn.sigmoid(zg[:, D_MODEL:2 * D_MODEL]) * _dot(o_c.astype(BF16), p_conv_ref[...])
              + jax.nn.sigmoid(zg[:, 2 * D_MODEL:3 * D_MODEL]) * _dot(o_m.astype(BF16), p_mem_ref[...]))
    mix = _dot(merged.astype(BF16), w_o_ref[...])
    return _layer_norm(alpha * x + mix, g_ref[...], b_ref[...])


def _const_spec(shape):
    n = len(shape)
    return pl.BlockSpec(shape, lambda *_: (0,) * n, pipeline_mode=pl.Buffered(1))


def _mixer_prompt_kernel(sink_ref, x_ref, mem_ref, w_in_ref, convw_ref, p_attn_ref, p_conv_ref, p_mem_ref,
                         w_o_ref, wmk_ref, wmv_ref, g_ref, b_ref,
                         x1_ref, wk_ref, wv_ref, cv_ref, mk_ref, mv_ref,
                         kprev, vprev, uprev, mk_s, mv_s, *, ts, alpha):
    s = pl.program_id(1)
    ns = pl.num_programs(1)
    x = x_ref[0]
    xb = x.astype(BF16)

    @pl.when(s == 0)
    def _():
        memb = mem_ref[0].astype(BF16)
        mk = _dot(memb, wmk_ref[...])
        mv = _dot(memb, wmv_ref[...])
        mk_ref[0] = mk.T
        mv_ref[0] = mv.T
        mk_s[...] = mk.astype(BF16)
        mv_s[...] = mv.astype(BF16)
        kprev[...] = jnp.zeros_like(kprev)
        vprev[...] = jnp.zeros_like(vprev)
        uprev[...] = jnp.zeros_like(uprev)

    z = _dot(xb, w_in_ref[:, 0:_MIX_COLS])
    q = z[:, _Q0:_K0] * ATTN_SCALE
    k = z[:, _K0:_V0]
    v = z[:, _V0:_H0]
    hc = z[:, _H0:_GB0]
    g_b = z[:, _GB0:_GC0]
    g_c = z[:, _GC0:_QM0]
    qm = z[:, _QM0:_GA0] * ATTN_SCALE

    k_ext = jnp.concatenate([kprev[...], k.astype(BF16)], axis=0)
    v_ext = jnp.concatenate([vprev[...], v.astype(BF16)], axis=0)
    row_i = lax.broadcasted_iota(jnp.int32, (WINDOW, 2 * WINDOW), 0)
    col_j = lax.broadcasted_iota(jnp.int32, (WINDOW, 2 * WINDOW), 1)
    band = (col_j > row_i) & (col_j <= row_i + WINDOW)
    band_first = band & ((col_j >= WINDOW) | (s > 0))
    o_blocks = []
    for blk in range(ts // WINDOW):
        qs = _stack_lane_halves(q[blk * WINDOW:(blk + 1) * WINDOW], GROUP).astype(BF16)
        kband = k_ext[blk * WINDOW:blk * WINDOW + 2 * WINDOW]
        vband = v_ext[blk * WINDOW:blk * WINDOW + 2 * WINDOW]
        sc = _dot_nt(qs, kband)
        mask = band_first if blk == 0 else band
        ps = []
        for hh in range(N_HEADS):
            sink_h = sink_ref[_stack_head(hh)]
            sch = jnp.where(mask, sc[hh * WINDOW:(hh + 1) * WINDOW], NEG_INF)
            m = jnp.maximum(jnp.max(sch, axis=-1, keepdims=True), sink_h)
            e = jnp.exp(sch - m)
            den = jnp.sum(e, axis=-1, keepdims=True) + jnp.exp(sink_h - m)
            ps.append((e / den).astype(BF16))
        r = _dot(jnp.concatenate(ps, axis=0), vband)
        o_blocks.append(_unstack_lane_halves(r, WINDOW, GROUP))
    o_a = jnp.concatenate(o_blocks, axis=0)
    kprev[...] = k_ext[ts:ts + WINDOW]
    vprev[...] = v_ext[ts:ts + WINDOW]

    u = g_c * hc
    rows = lax.broadcasted_iota(jnp.int32, (ts, CONV_DIM), 0)
    up1 = uprev[7:8, :]
    up2 = uprev[6:7, :]
    u_m1 = jnp.where(rows == 0, up1, pltpu.roll(u, 1, 0))
    u_m2 = jnp.where(rows == 0, up2, jnp.where(rows == 1, up1, pltpu.roll(u, 2, 0)))
    cw = convw_ref[...]
    conv = cw[0:1, :] * u_m2 + cw[1:2, :] * u_m1 + cw[2:3, :] * u
    o_c = g_b * conv
    uprev[...] = u[ts - 8:ts]

    qms = _stack_lane_halves(qm, MEM_HEADS // 2).astype(BF16)
    o_cols = []
    lo = lax.broadcasted_iota(jnp.int32, (ts, LANES), 1) < HEAD_DIM
    for jb in range(MEM_HEADS // 2):
        scm = _dot_nt(qms[2 * jb * ts:(2 * jb + 2) * ts], mk_s[:, jb * LANES:(jb + 1) * LANES])
        mm = jnp.max(scm, axis=-1, keepdims=True)
        em = jnp.exp(scm - mm)
        pm = (em / jnp.sum(em, axis=-1, keepdims=True)).astype(BF16)
        rm = _dot(pm, mv_s[:, jb * LANES:(jb + 1) * LANES])
        o_cols.append(jnp.where(lo, rm[0:ts], rm[ts:2 * ts]))
    o_m = jnp.concatenate(o_cols, axis=1)

    zg = _dot(xb, w_in_ref[:, _GA0:IN_DIM])
    x1_ref[0] = _merge_project_norm(x, zg, o_a, o_c, o_m, p_attn_ref, p_conv_ref, p_mem_ref, w_o_ref,
                                    g_ref, b_ref, alpha)

    @pl.when(s == ns - 1)
    def _():
        wk_ref[0] = k[ts - WINDOW:ts].T
        wv_ref[0] = v[ts - WINDOW:ts].T
        cv_ref[0] = u[ts - (CONV_WIDTH - 1):ts]


def _mixer_prompt(x, mem, w, l, *, ts, alpha):
    b, seq, _ = x.shape
    kern = functools.partial(_mixer_prompt_kernel, ts=ts, alpha=alpha)
    smem = pl.BlockSpec(memory_space=pltpu.SMEM)
    in_specs = [
        smem,
        pl.BlockSpec((1, ts, D_MODEL), lambda i, s: (i, s, 0)),
        pl.BlockSpec((1, N_MEM, D_MODEL), lambda i, s: (i, 0, 0)),
        _const_spec((D_MODEL, IN_DIM)),
        _const_spec((CONV_WIDTH, CONV_DIM)),
        _const_spec((ATTN_DIM, D_MODEL)),
        _const_spec((CONV_DIM, D_MODEL)),
        _const_spec((MEM_DIM, D_MODEL)),
        _const_spec((D_MODEL, D_MODEL)),
        _const_spec((D_MODEL, MEM_DIM)),
        _const_spec((D_MODEL, MEM_DIM)),
        _const_spec((1, D_MODEL)),
        _const_spec((1, D_MODEL)),
    ]
    out_shape = (
        jax.ShapeDtypeStruct((b, seq, D_MODEL), F32),
        jax.ShapeDtypeStruct((b, KV_DIM, WINDOW), F32),
        jax.ShapeDtypeStruct((b, KV_DIM, WINDOW), F32),
        jax.ShapeDtypeStruct((b, CONV_WIDTH - 1, CONV_DIM), F32),
        jax.ShapeDtypeStruct((b, MEM_DIM, N_MEM), F32),
        jax.ShapeDtypeStruct((b, MEM_DIM, N_MEM), F32),
    )
    out_specs = (
        pl.BlockSpec((1, ts, D_MODEL), lambda i, s: (i, s, 0)),
        pl.BlockSpec((1, KV_DIM, WINDOW), lambda i, s: (i, 0, 0)),
        pl.BlockSpec((1, KV_DIM, WINDOW), lambda i, s: (i, 0, 0)),
        pl.BlockSpec((1, CONV_WIDTH - 1, CONV_DIM), lambda i, s: (i, 0, 0)),
        pl.BlockSpec((1, MEM_DIM, N_MEM), lambda i, s: (i, 0, 0)),
        pl.BlockSpec((1, MEM_DIM, N_MEM), lambda i, s: (i, 0, 0)),
    )
    scratch = [
        pltpu.VMEM((WINDOW, KV_DIM), BF16),
        pltpu.VMEM((WINDOW, KV_DIM), BF16),
        pltpu.VMEM((8, CONV_DIM), F32),
        pltpu.VMEM((N_MEM, MEM_DIM), BF16),
        pltpu.VMEM((N_MEM, MEM_DIM), BF16),
    ]
    return pl.pallas_call(
        kern,
        grid=(b, seq // ts),
        in_specs=in_specs,
        out_specs=out_specs,
        out_shape=out_shape,
        scratch_shapes=scratch,
        compiler_params=pltpu.CompilerParams(dimension_semantics=("arbitrary", "arbitrary"),
                                             vmem_limit_bytes=VMEM_LIMIT),
        name=f"mixer_prompt_l{l}",
    )(w["sink"], x, mem, w["w_in"], w["conv_w"], w["p_attn"], w["p_conv"], w["p_mem"], w["w_o"],
      w["w_mem_k"], w["w_mem_v"], w["ln1_g"], w["ln1_b"])


def _mixer_sample_kernel(sink_ref, x_ref, ck_ref, cvv_ref, cc_ref, cmk_ref, cmv_ref, w_in_ref, wkvt_ref, convw_ref,
                         p_attn_ref, p_conv_ref, p_mem_ref, w_o_ref, g_ref, b_ref,
                         x1_ref, wk_ref, wv_ref, cv_ref, *, bb, t, alpha):
    nbuf = ck_ref.shape[2]
    x = x_ref[...]
    xb = x.astype(BF16)
    z = _dot(xb, w_in_ref[:, 0:_MIX_COLS])
    q = z[:, _Q0:_K0] * ATTN_SCALE
    u = z[:, _GC0:_QM0] * z[:, _H0:_GB0]
    g_b = z[:, _GB0:_GC0]
    qm = z[:, _QM0:_GA0] * ATTN_SCALE
    cw = convw_ref[...]
    kvt = _dot_nt(wkvt_ref[...], xb)
    kt_new = kvt[0:KV_DIM]
    vt_new = kvt[KV_DIM:2 * KV_DIM]

    nstk = N_HEADS * t
    r_t = lax.broadcasted_iota(jnp.int32, (nstk, 2 * WINDOW), 0) % t
    c_j = lax.broadcasted_iota(jnp.int32, (nstk, 2 * WINDOW), 1)
    dist = r_t + nbuf - c_j
    wmask = (dist >= 0) & (dist < WINDOW) & (c_j < nbuf + t)
    blk_id = lax.broadcasted_iota(jnp.int32, (nstk, 1), 0) // t
    sink_col = jnp.zeros((nstk, 1), F32)
    for hh in range(N_HEADS):
        sink_col = jnp.where(blk_id == hh, sink_ref[_stack_head(hh)], sink_col)
    mlane = lax.broadcasted_iota(jnp.int32, (t, MEM_DIM), 1) // HEAD_DIM
    klane = lax.broadcasted_iota(jnp.int32, (KV_DIM, LANES), 1)

    o_a_rows, o_c_rows, o_m_rows = [], [], []
    for bi in range(bb):
        sl = slice(bi * t, (bi + 1) * t)
        kc = ck_ref[bi]
        vc = cvv_ref[bi]
        k_new0 = jnp.where(klane < t, pltpu.roll(kt_new, (LANES - bi * t) % LANES, 1), 0.0)
        v_new0 = jnp.where(klane < t, pltpu.roll(vt_new, (LANES - bi * t) % LANES, 1), 0.0)
        k_all = jnp.concatenate([kc, k_new0], axis=1).astype(BF16)
        v_all = jnp.concatenate([vc, v_new0], axis=1).astype(BF16)
        qs = _stack_lane_halves(q[sl], GROUP).astype(BF16)
        sc = jnp.where(wmask, _dot(qs, k_all), NEG_INF)
        m = jnp.maximum(jnp.max(sc, axis=-1, keepdims=True), sink_col)
        e = jnp.exp(sc - m)
        den = jnp.sum(e, axis=-1, keepdims=True) + jnp.exp(sink_col - m)
        r = _dot_nt((e / den).astype(BF16), v_all)
        o_a_rows.append(_unstack_lane_halves(r, t, GROUP))
        tail = (nbuf - t - bi * t) % LANES
        wk_ref[bi] = jnp.where(klane < nbuf - t, pltpu.roll(kc, nbuf - t, 1), pltpu.roll(kt_new, tail, 1))
        wv_ref[bi] = jnp.where(klane < nbuf - t, pltpu.roll(vc, nbuf - t, 1), pltpu.roll(vt_new, tail, 1))
        u_ext = jnp.concatenate([cc_ref[bi], u[sl]], axis=0)
        conv = cw[0:1, :] * u_ext[0:t] + cw[1:2, :] * u_ext[1:t + 1] + cw[2:3, :] * u_ext[2:t + 2]
        o_c_rows.append(g_b[sl] * conv)
        cv_ref[bi] = u_ext[t:t + CONV_WIDTH - 1]
        qmb = qm[sl]
        qms = jnp.concatenate([jnp.where(mlane == hh, qmb, 0.0) for hh in range(MEM_HEADS)], axis=0).astype(BF16)
        scm = _dot(qms, cmk_ref[bi].astype(BF16))
        mm = jnp.max(scm, axis=-1, keepdims=True)
        em = jnp.exp(scm - mm)
        pm = (em / jnp.sum(em, axis=-1, keepdims=True)).astype(BF16)
        rm = _dot_nt(pm, cmv_ref[bi].astype(BF16))
        om = jnp.zeros((t, MEM_DIM), F32)
        for hh in range(MEM_HEADS):
            om = jnp.where(mlane == hh, rm[hh * t:(hh + 1) * t], om)
        o_m_rows.append(om)

    o_a = jnp.concatenate(o_a_rows, axis=0)
    o_c = jnp.concatenate(o_c_rows, axis=0)
    o_m = jnp.concatenate(o_m_rows, axis=0)
    zg = _dot(xb, w_in_ref[:, _GA0:IN_DIM])
    x1_ref[...] = _merge_project_norm(x, zg, o_a, o_c, o_m, p_attn_ref, p_conv_ref, p_mem_ref, w_o_ref,
                                      g_ref, b_ref, alpha)


def _mixer_sample(x2d, ckt, cvt, cc, cmkt, cmvt, w, l, *, bb, t, alpha):
    nb, nbuf = ckt.shape[1], ckt.shape[3]
    assert nbuf == LANES and bb * t == LANES and t % 8 == 0
    kern = functools.partial(_mixer_sample_kernel, bb=bb, t=t, alpha=alpha)
    smem = pl.BlockSpec(memory_space=pltpu.SMEM)

    def cache_spec(d1, d2):
        return pl.BlockSpec((None, bb, d1, d2), lambda i: (l, i, 0, 0))

    in_specs = [
        smem,
        pl.BlockSpec((bb * t, D_MODEL), lambda i: (i, 0)),
        cache_spec(KV_DIM, nbuf),
        cache_spec(KV_DIM, nbuf),
        cache_spec(CONV_WIDTH - 1, CONV_DIM),
        cache_spec(MEM_DIM, N_MEM),
        cache_spec(MEM_DIM, N_MEM),
        _const_spec((D_MODEL, IN_DIM)),
        _const_spec((2 * KV_DIM, D_MODEL)),
        _const_spec((CONV_WIDTH, CONV_DIM)),
        _const_spec((ATTN_DIM, D_MODEL)),
        _const_spec((CONV_DIM, D_MODEL)),
        _const_spec((MEM_DIM, D_MODEL)),
        _const_spec((D_MODEL, D_MODEL)),
        _const_spec((1, D_MODEL)),
        _const_spec((1, D_MODEL)),
    ]
    out_shape = (
        jax.ShapeDtypeStruct((nb * t, D_MODEL), F32),
        jax.ShapeDtypeStruct((nb, KV_DIM, nbuf), F32),
        jax.ShapeDtypeStruct((nb, KV_DIM, nbuf), F32),
        jax.ShapeDtypeStruct((nb, CONV_WIDTH - 1, CONV_DIM), F32),
    )
    out_specs = (
        pl.BlockSpec((bb * t, D_MODEL), lambda i: (i, 0)),
        pl.BlockSpec((bb, KV_DIM, nbuf), lambda i: (i, 0, 0)),
        pl.BlockSpec((bb, KV_DIM, nbuf), lambda i: (i, 0, 0)),
        pl.BlockSpec((bb, CONV_WIDTH - 1, CONV_DIM), lambda i: (i, 0, 0)),
    )
    return pl.pallas_call(
        kern,
        grid=(nb // bb,),
        in_specs=in_specs,
        out_specs=out_specs,
        out_shape=out_shape,
        compiler_params=pltpu.CompilerParams(dimension_semantics=("arbitrary",),
                                             vmem_limit_bytes=VMEM_LIMIT),
        name=f"mixer_sample_l{l}",
    )(w["sink"], x2d, ckt, cvt, cc, cmkt, cmvt, w["w_in"], w["w_kvt"], w["conv_w"], w["p_attn"], w["p_conv"],
      w["p_mem"], w["w_o"], w["ln1_g"], w["ln1_b"])


def _swiglu(xb, wg, wu, wd):
    g = _dot(xb, wg)
    u = _dot(xb, wu)
    return _dot((jax.nn.silu(g) * u).astype(BF16), wd)


def _ffn_kernel(x_ref, wg_ref, wu_ref, wd_ref, g_ref, b_ref, o_ref, *, alpha):
    x = x_ref[...]
    y = _swiglu(x.astype(BF16), wg_ref[...], wu_ref[...], wd_ref[...])
    o_ref[...] = _layer_norm(alpha * x + y, g_ref[...], b_ref[...])


def _ffn_dense(x, wg, wu, wd, g, b, *, tm, alpha, name):
    n = x.shape[0]
    return pl.pallas_call(
        functools.partial(_ffn_kernel, alpha=alpha),
        grid=(n // tm,),
        in_specs=[pl.BlockSpec((tm, D_MODEL), lambda i: (i, 0)),
                  _const_spec((D_MODEL, D_FF)), _const_spec((D_MODEL, D_FF)), _const_spec((D_FF, D_MODEL)),
                  _const_spec((1, D_MODEL)), _const_spec((1, D_MODEL))],
        out_specs=pl.BlockSpec((tm, D_MODEL), lambda i: (i, 0)),
        out_shape=jax.ShapeDtypeStruct((n, D_MODEL), F32),
        compiler_params=pltpu.CompilerParams(dimension_semantics=("arbitrary",), vmem_limit_bytes=VMEM_LIMIT),
        name=name,
    )(x, wg, wu, wd, g, b)


def _two_group_specs(tm, n_a_tiles, n_b_tiles):
    spec_a = pl.BlockSpec((tm, D_MODEL), lambda i: (jnp.minimum(i, n_a_tiles - 1), 0))
    spec_b = pl.BlockSpec((tm, D_MODEL), lambda i: (jnp.clip(i - n_a_tiles, 0, n_b_tiles - 1), 0))
    return spec_a, spec_b


def _route_kernel(xa_ref, xb_ref, wrt_ref, rb_ref, tri_ref, slab_ref, cnt_ref, carry, *, tm, n_a_tiles):
    i = pl.program_id(0)

    @pl.when(i == 0)
    def _():
        carry[...] = jnp.zeros_like(carry)

    x = jnp.where(i < n_a_tiles, xa_ref[...], xb_ref[...])
    lg = _dot_nt(wrt_ref[...], x.astype(BF16)) + rb_ref[...]
    e_iota = lax.broadcasted_iota(jnp.int32, (N_EXPERTS, tm), 0)
    m1 = jnp.max(lg, axis=0, keepdims=True)
    i1 = jnp.min(jnp.where(lg == m1, e_iota, N_EXPERTS), axis=0, keepdims=True)
    lg2 = jnp.where(e_iota == i1, NEG_INF, lg)
    m2 = jnp.max(lg2, axis=0, keepdims=True)
    i2 = jnp.min(jnp.where(lg2 == m2, e_iota, N_EXPERTS), axis=0, keepdims=True)
    e2 = jnp.exp(m2 - m1)
    den = 1.0 + e2
    w1 = 1.0 / den
    w2 = e2 / den
    sel1 = e_iota == i1
    sel2 = e_iota == i2
    oh = jnp.where(sel1 | sel2, 1.0, 0.0)
    pre = _dot(oh.astype(BF16), tri_ref[...]) + carry[:, 0:1]
    r1 = jnp.sum(jnp.where(sel1, pre, 0.0), axis=0, keepdims=True)
    r2 = jnp.sum(jnp.where(sel2, pre, 0.0), axis=0, keepdims=True)
    carry[...] = carry[...] + jnp.sum(oh, axis=1, keepdims=True)
    rows = [i1.astype(F32), i2.astype(F32), r1, r2, w1, w2]
    slab = jnp.zeros((N_EXPERTS, tm), F32)
    for ri, rv in enumerate(rows):
        slab = jnp.where(e_iota == ri, rv, slab)
    slab_ref[...] = slab
    cnt_ref[...] = carry[...]


def _route(xa, xb, wrt, rb, *, tm):
    na, nb_ = xa.shape[0] // tm, xb.shape[0] // tm
    tri = jnp.triu(jnp.ones((tm, tm), BF16), k=1)
    spec_a, spec_b = _two_group_specs(tm, na, nb_)
    return pl.pallas_call(
        functools.partial(_route_kernel, tm=tm, n_a_tiles=na),
        grid=(na + nb_,),
        in_specs=[spec_a, spec_b,
                  _const_spec((N_EXPERTS, D_MODEL)), _const_spec((N_EXPERTS, 1)), _const_spec((tm, tm))],
        out_specs=(pl.BlockSpec((N_EXPERTS, tm), lambda i: (0, i)),
                   pl.BlockSpec((N_EXPERTS, LANES), lambda i: (0, 0))),
        out_shape=(jax.ShapeDtypeStruct((N_EXPERTS, (na + nb_) * tm), F32),
                   jax.ShapeDtypeStruct((N_EXPERTS, LANES), F32)),
        scratch_shapes=[pltpu.VMEM((N_EXPERTS, LANES), F32)],
        compiler_params=pltpu.CompilerParams(dimension_semantics=("arbitrary",)),
        name="moe_route",
    )(xa, xb, wrt, rb, tri)


def _row_copy(src, src_row, dst, dst_row, sem):
    return pltpu.make_async_copy(src.at[pl.ds(src_row, 1)], dst.at[pl.ds(dst_row, 1)], sem)


def _dispatch_kernel(slot_ref, xa_ref, xb_ref, z_ref, xs_ref, sem, *, tm, n_a_tiles, n_b_tiles):
    i = pl.program_id(0)

    def issue_from(src_ref):
        def issue(r, c):
            _row_copy(src_ref, r, xs_ref, slot_ref[0, 0, 2 * r], sem).start(priority=0)
            _row_copy(src_ref, r, xs_ref, slot_ref[0, 0, 2 * r + 1], sem).start(priority=1)
            return c

        lax.fori_loop(0, tm, issue, 0, unroll=8)

    @pl.when(i < n_a_tiles)
    def _():
        issue_from(xa_ref)

    @pl.when((i >= n_a_tiles) & (i < n_a_tiles + n_b_tiles))
    def _():
        issue_from(xb_ref)

    @pl.when(i >= n_a_tiles + n_b_tiles)
    def _():
        issue_from(z_ref)

    def drain(r, c):
        _row_copy(z_ref, 0, xs_ref, 0, sem).wait()
        _row_copy(z_ref, 0, xs_ref, 0, sem).wait()
        return c

    lax.fori_loop(0, tm, drain, 0, unroll=8)


def _dispatch(xa, xb, slots3, n_rows, *, tm):
    na, nb_ = xa.shape[0] // tm, xb.shape[0] // tm
    nz = slots3.shape[0] - na - nb_
    spec_a, spec_b = _two_group_specs(tm, na, nb_)
    zsrc = jnp.zeros((tm, D_MODEL), F32)
    return pl.pallas_call(
        functools.partial(_dispatch_kernel, tm=tm, n_a_tiles=na, n_b_tiles=nb_),
        grid=(na + nb_ + nz,),
        in_specs=[pl.BlockSpec((1, 1, 2 * tm), lambda i: (i, 0, 0), memory_space=pltpu.SMEM),
                  spec_a, spec_b, _const_spec((tm, D_MODEL))],
        out_specs=pl.BlockSpec(memory_space=pl.ANY),
        out_shape=jax.ShapeDtypeStruct((n_rows, D_MODEL), F32),
        scratch_shapes=[pltpu.SemaphoreType.DMA(())],
        compiler_params=pltpu.CompilerParams(dimension_semantics=("arbitrary",)),
        name="moe_dispatch",
    )(slots3, xa, xb, zsrc)


def _expert_kernel(te_ref, nt_ref, xs_ref, wg_ref, wu_ref, wd_ref, o_ref):
    del te_ref
    live = pl.program_id(0) < nt_ref[0]

    @pl.when(live)
    def _():
        o_ref[...] = _swiglu(xs_ref[...].astype(BF16), wg_ref[0], wu_ref[0], wd_ref[0])

    @pl.when(jnp.logical_not(live))
    def _():
        o_ref[...] = jnp.zeros_like(o_ref)


def _experts(xs, tile_expert, n_tiles, wg, wu, wd, *, tm):
    n_rows = xs.shape[0]

    def row_map(i, te, nt):
        return (jnp.minimum(i, nt[0] - 1), 0)

    def w_map(i, te, nt):
        return (te[i], 0, 0)

    return pl.pallas_call(
        _expert_kernel,
        grid_spec=pltpu.PrefetchScalarGridSpec(
            num_scalar_prefetch=2,
            grid=(n_rows // tm,),
            in_specs=[pl.BlockSpec((tm, D_MODEL), row_map),
                      pl.BlockSpec((1, D_MODEL, D_FF), w_map),
                      pl.BlockSpec((1, D_MODEL, D_FF), w_map),
                      pl.BlockSpec((1, D_FF, D_MODEL), w_map)],
            out_specs=pl.BlockSpec((tm, D_MODEL), lambda i, te, nt: (i, 0)),
        ),
        out_shape=jax.ShapeDtypeStruct((n_rows, D_MODEL), F32),
        compiler_params=pltpu.CompilerParams(dimension_semantics=("arbitrary",), vmem_limit_bytes=VMEM_LIMIT),
        name="moe_experts",
    )(tile_expert, n_tiles, xs, wg, wu, wd)


def _combine_kernel(slot_ref, x_ref, gate_ref, ys_ref, g_ref, b_ref, o_ref, buf0, buf1, sem, *, tm, alpha):
    def issue(r, c):
        _row_copy(ys_ref, slot_ref[0, 0, 2 * r], buf0, r, sem).start(priority=0)
        _row_copy(ys_ref, slot_ref[0, 0, 2 * r + 1], buf1, r, sem).start(priority=1)
        return c

    lax.fori_loop(0, tm, issue, 0, unroll=8)

    def drain(r, c):
        _row_copy(ys_ref, 0, buf0, 0, sem).wait()
        _row_copy(ys_ref, 0, buf1, 0, sem).wait()
        return c

    lax.fori_loop(0, tm, drain, 0, unroll=8)
    gates = gate_ref[...]
    y = gates[:, 0:1] * buf0[...] + gates[:, 1:2] * buf1[...]
    o_ref[...] = _layer_norm(alpha * x_ref[...] + y, g_ref[...], b_ref[...])


def _combine(x, slots3, gates, ys, g, b, *, tm, alpha, name):
    n = x.shape[0]
    return pl.pallas_call(
        functools.partial(_combine_kernel, tm=tm, alpha=alpha),
        grid=(n // tm,),
        in_specs=[pl.BlockSpec((1, 1, 2 * tm), lambda i: (i, 0, 0), memory_space=pltpu.SMEM),
                  pl.BlockSpec((tm, D_MODEL), lambda i: (i, 0)),
                  pl.BlockSpec((tm, 2), lambda i: (i, 0)),
                  pl.BlockSpec(memory_space=pl.ANY),
                  _const_spec((1, D_MODEL)), _const_spec((1, D_MODEL))],
        out_specs=pl.BlockSpec((tm, D_MODEL), lambda i: (i, 0)),
        out_shape=jax.ShapeDtypeStruct((n, D_MODEL), F32),
        scratch_shapes=[pltpu.VMEM((tm, D_MODEL), F32), pltpu.VMEM((tm, D_MODEL), F32),
                        pltpu.SemaphoreType.DMA(())],
        compiler_params=pltpu.CompilerParams(dimension_semantics=("arbitrary",)),
        name=name,
    )(slots3, x, gates, ys, g, b)


def _free_slots(counts, offs, padded, n_rows, n_free):
    starts = jnp.concatenate([offs + counts, (offs[-1:] + padded[-1:])])
    gaps = jnp.concatenate([padded - counts, n_rows - (offs[-1:] + padded[-1:])])
    cum = jnp.cumsum(gaps)
    prev = cum - gaps
    idx = jnp.arange(n_free, dtype=jnp.int32)
    seg = jnp.sum(idx[:, None] >= cum[None, :], axis=1)
    pick = seg[:, None] == jnp.arange(N_EXPERTS + 1, dtype=jnp.int32)[None, :]
    return jnp.sum(jnp.where(pick, (starts - prev)[None, :], 0), axis=1) + idx


def _moe(xa, xb, router_w, router_b, wg, wu, wd, g, b, *, alpha, tm_rows, tm_expert):
    na, nb_ = xa.shape[0], xb.shape[0]
    n = na + nb_
    slab, cnt = _route(xa, xb, router_w.T.astype(BF16), router_b.reshape(N_EXPERTS, 1), tm=tm_rows)
    counts = cnt[:, 0].astype(jnp.int32)
    tiles_per = (counts + tm_expert - 1) // tm_expert
    padded = tiles_per * tm_expert
    ends = jnp.cumsum(tiles_per)
    offs = (ends - tiles_per) * tm_expert
    i1 = slab[0].astype(jnp.int32)
    i2 = slab[1].astype(jnp.int32)
    eye = jnp.arange(N_EXPERTS, dtype=jnp.int32)[:, None]
    off1 = jnp.sum(jnp.where(eye == i1[None, :], offs[:, None], 0), axis=0)
    off2 = jnp.sum(jnp.where(eye == i2[None, :], offs[:, None], 0), axis=0)
    slots = jnp.stack([off1 + slab[2].astype(jnp.int32), off2 + slab[3].astype(jnp.int32)], axis=1)
    gates = jnp.stack([slab[4], slab[5]], axis=1)
    n_free = N_EXPERTS * tm_expert
    n_rows = 2 * n + n_free
    assert n_free % (2 * tm_rows) == 0
    free = _free_slots(counts, offs, padded, n_rows, n_free)
    max_tiles = n_rows // tm_expert
    tile_expert = jnp.minimum(
        jnp.sum(jnp.arange(max_tiles, dtype=jnp.int32)[:, None] >= ends[None, :], axis=1), N_EXPERTS - 1
    ).astype(jnp.int32)
    n_tiles = ends[-1:].astype(jnp.int32)
    slots3 = jnp.concatenate([slots.reshape(-1), free]).reshape(-1, 1, 2 * tm_rows)

    xs = _dispatch(xa, xb, slots3, n_rows, tm=tm_rows)
    ys = _experts(xs, tile_expert, n_tiles, wg, wu, wd, tm=tm_expert)
    ta = na // tm_rows
    ya = _combine(xa, slots3[:ta], gates[:na], ys, g, b, tm=tm_rows, alpha=alpha, name="moe_combine_a")
    yb = _combine(xb, slots3[ta:ta + nb_ // tm_rows], gates[na:], ys, g, b, tm=tm_rows, alpha=alpha,
                  name="moe_combine_b")
    return ya, yb


def _layer_weights(l, w_in, conv_w, sink, p_attn, p_conv, p_mem, w_o, w_mem_k, w_mem_v, ln1_g, ln1_b):
    w_in_l = w_in[l].astype(BF16)
    w_q = w_in_l[:, :ATTN_DIM].reshape(D_MODEL, N_KV_HEADS, GROUP, HEAD_DIM)
    w_q = jnp.transpose(w_q, (0, 2, 1, 3)).reshape(D_MODEL, ATTN_DIM)
    p_a = p_attn[l].astype(BF16).reshape(N_KV_HEADS, GROUP, HEAD_DIM, D_MODEL)
    p_a = jnp.transpose(p_a, (1, 0, 2, 3)).reshape(ATTN_DIM, D_MODEL)
    return dict(
        sink=sink[l],
        w_in=jnp.concatenate([w_q, w_in_l[:, ATTN_DIM:]], axis=1),
        w_kvt=w_in_l[:, _K0:_H0].T,
        conv_w=conv_w[l],
        p_attn=p_a,
        p_conv=p_conv[l].astype(BF16),
        p_mem=p_mem[l].astype(BF16),
        w_o=w_o[l].astype(BF16),
        w_mem_k=w_mem_k[l].astype(BF16),
        w_mem_v=w_mem_v[l].astype(BF16),
        ln1_g=ln1_g[l].reshape(1, D_MODEL),
        ln1_b=ln1_b[l].reshape(1, D_MODEL),
    )


def _feature_major(c):
    d, nb, tok, h, hd = c.shape
    return jnp.transpose(c, (0, 1, 3, 4, 2)).reshape(d, nb, h * hd, tok)


def _token_major(c, heads):
    d, nb, f, tok = c.shape
    return jnp.transpose(c.reshape(d, nb, heads, f // heads, tok), (0, 1, 4, 2, 3))


def kernel(x_prompt, x_sample, mem_prompt, cache_win_k, cache_win_v, cache_conv, cache_mem_k, cache_mem_v, w_in, conv_w, sink, p_attn, p_conv, p_mem, w_o, w_mem_k, w_mem_v, ln1_g, ln1_b, ln2_g, ln2_b, ffn_w_gate, ffn_w_up, ffn_w_down, router_w, router_b, exp_w_gate, exp_w_up, exp_w_down):
    depth = w_in.shape[0]
    alpha = (2 * depth) ** 0.25
    b, seq, _ = x_prompt.shape
    nb, t, _ = x_sample.shape
    n_p = b * seq
    n_s = nb * t

    xp = x_prompt
    xs = x_sample.reshape(n_s, D_MODEL)
    ckt = _feature_major(cache_win_k)
    cvt = _feature_major(cache_win_v)
    cmkt = _feature_major(cache_mem_k)
    cmvt = _feature_major(cache_mem_v)

    outs = {k_: [] for k_ in ("wk_p", "wv_p", "cv_p", "mk_p", "mv_p", "wk_s", "wv_s", "cv_s")}
    for l in range(depth):
        w = _layer_weights(l, w_in, conv_w, sink, p_attn, p_conv, p_mem, w_o, w_mem_k, w_mem_v, ln1_g, ln1_b)
        x1p, wk, wv, cvp, mk, mv = _mixer_prompt(xp, mem_prompt, w, l, ts=512, alpha=alpha)
        x1s, wks, wvs, cvs = _mixer_sample(xs, ckt, cvt, cache_conv, cmkt, cmvt, w, l, bb=LANES // t, t=t, alpha=alpha)
        outs["wk_p"].append(wk); outs["wv_p"].append(wv); outs["cv_p"].append(cvp)
        outs["mk_p"].append(mk); outs["mv_p"].append(mv)
        outs["wk_s"].append(wks); outs["wv_s"].append(wvs); outs["cv_s"].append(cvs)
        x1p = x1p.reshape(n_p, D_MODEL)
        g2 = ln2_g[l].reshape(1, D_MODEL)
        b2 = ln2_b[l].reshape(1, D_MODEL)
        i = l // 2
        if l % 2 == 0:
            wg, wu, wd = ffn_w_gate[i].astype(BF16), ffn_w_up[i].astype(BF16), ffn_w_down[i].astype(BF16)
            x2p = _ffn_dense(x1p, wg, wu, wd, g2, b2, tm=512, alpha=alpha, name=f"ffn_dense_p_l{l}")
            xs = _ffn_dense(x1s, wg, wu, wd, g2, b2, tm=512, alpha=alpha, name=f"ffn_dense_s_l{l}")
        else:
            x2p, xs = _moe(x1p, x1s, router_w[i], router_b[i], exp_w_gate[i].astype(BF16), exp_w_up[i].astype(BF16),
                           exp_w_down[i].astype(BF16), g2, b2, alpha=alpha, tm_rows=512, tm_expert=256)
        xp = x2p.reshape(b, seq, D_MODEL)

    return (xp, xs.reshape(nb, t, D_MODEL),
            _token_major(jnp.stack(outs["wk_p"]), N_KV_HEADS),
            _token_major(jnp.stack(outs["wv_p"]), N_KV_HEADS),
            jnp.stack(outs["cv_p"]),
            _token_major(jnp.stack(outs["mk_p"]), MEM_HEADS),
            _token_major(jnp.stack(outs["mv_p"]), MEM_HEADS),
            _token_major(jnp.stack(outs["wk_s"]), N_KV_HEADS),
            _token_major(jnp.stack(outs["wv_s"]), N_KV_HEADS),
            jnp.stack(outs["cv_s"]))
```

```python
import functools

import jax
import jax.numpy as jnp
from jax import lax
from jax.experimental import pallas as pl
from jax.experimental.pallas import tpu as pltpu

F32 = jnp.float32
BF16 = jnp.bfloat16

D_MODEL = 1024
HEAD_DIM = 64
N_HEADS = 8
N_KV_HEADS = 2
GROUP = N_HEADS // N_KV_HEADS
ATTN_DIM = N_HEADS * HEAD_DIM
KV_DIM = N_KV_HEADS * HEAD_DIM
WINDOW = 128
ATTN_SCALE = HEAD_DIM ** -0.5
CONV_DIM = 256
CONV_WIDTH = 3
N_MEM = 256
MEM_HEADS = 4
MEM_DIM = MEM_HEADS * HEAD_DIM
D_FF = 2816
N_EXPERTS = 8
LN_EPS = 1e-5

_Q0, _K0, _V0, _H0, _GB0, _GC0, _QM0, _GA0 = 0, 512, 640, 768, 1024, 1280, 1536, 1792
_MIX_COLS = _GA0
IN_DIM = _GA0 + 3 * D_MODEL

LANES = 128
VMEM_LIMIT = 56 * 1024 * 1024

NEG_INF = float("-inf")


def _stack_head(s):
    return (s // 2) if s % 2 == 0 else GROUP + s // 2


def _layer_norm(v, g, b):
    mu = jnp.mean(v, axis=-1, keepdims=True)
    c = v - mu
    var = jnp.mean(c * c, axis=-1, keepdims=True)
    return c * lax.rsqrt(var + LN_EPS) * g + b


def _dot(a, b):
    return jnp.dot(a, b, preferred_element_type=F32)


def _dot_nt(a, b):
    return lax.dot_general(a, b, (((1,), (1,)), ((), ())), preferred_element_type=F32)


def _stack_lane_halves(q, n_blocks):
    m = q.shape[0]
    lo = lax.broadcasted_iota(jnp.int32, (m, LANES), 1) < HEAD_DIM
    parts = []
    for j in range(n_blocks):
        blk = q[:, j * LANES:(j + 1) * LANES]
        parts.append(jnp.where(lo, blk, 0.0))
        parts.append(jnp.where(lo, 0.0, blk))
    return jnp.concatenate(parts, axis=0)


def _unstack_lane_halves(r, m, n_blocks):
    lo = lax.broadcasted_iota(jnp.int32, (m, LANES), 1) < HEAD_DIM
    cols = [jnp.where(lo, r[(2 * j) * m:(2 * j + 1) * m], r[(2 * j + 1) * m:(2 * j + 2) * m])
            for j in range(n_blocks)]
    return jnp.concatenate(cols, axis=1)


def _merge_project_norm(x, zg, o_a, o_c, o_m, p_attn_ref, p_conv_ref, p_mem_ref, w_o_ref, g_ref, b_ref, alpha):
    merged = (jax.nn.sigmoid(zg[:, 0:D_MODEL]) * _dot(o_a.astype(BF16), p_attn_ref[...])
              + jax.nn.sigmoid(zg[:, D_MODEL:2 * D_MODEL]) * _dot(o_c.astype(BF16), p_conv_ref[...])
              + jax.nn.sigmoid(zg[:, 2 * D_MODEL:3 * D_MODEL]) * _dot(o_m.astype(BF16), p_mem_ref[...]))
    mix = _dot(merged.astype(BF16), w_o_ref[...])
    return _layer_norm(alpha * x + mix, g_ref[...], b_ref[...])


def _const_spec(shape):
    n = len(shape)
    return pl.BlockSpec(shape, lambda *_: (0,) * n, pipeline_mode=pl.Buffered(1))


def _mixer_prompt_kernel(sink_ref, x_ref, mem_ref, w_in_ref, convw_ref, p_attn_ref, p_conv_ref, p_mem_ref,
                         w_o_ref, wmk_ref, wmv_ref, g_ref, b_ref,
                         x1_ref, wk_ref, wv_ref, cv_ref, mk_ref, mv_ref,
                         kprev, vprev, uprev, mk_s, mv_s, *, ts, alpha):
    s = pl.program_id(1)
    ns = pl.num_programs(1)
    x = x_ref[0]
    xb = x.astype(BF16)

    @pl.when(s == 0)
    def _():
        memb = mem_ref[0].astype(BF16)
        mk = _dot(memb, wmk_ref[...])
        mv = _dot(memb, wmv_ref[...])
        mk_ref[0] = mk.T
        mv_ref[0] = mv.T
        mk_s[...] = mk.astype(BF16)
        mv_s[...] = mv.astype(BF16)
        kprev[...] = jnp.zeros_like(kprev)
        vprev[...] = jnp.zeros_like(vprev)
        uprev[...] = jnp.zeros_like(uprev)

    z = _dot(xb, w_in_ref[:, 0:_MIX_COLS])
    q = z[:, _Q0:_K0] * ATTN_SCALE
    k = z[:, _K0:_V0]
    v = z[:, _V0:_H0]
    hc = z[:, _H0:_GB0]
    g_b = z[:, _GB0:_GC0]
    g_c = z[:, _GC0:_QM0]
    qm = z[:, _QM0:_GA0] * ATTN_SCALE

    k_ext = jnp.concatenate([kprev[...], k.astype(BF16)], axis=0)
    v_ext = jnp.concatenate([vprev[...], v.astype(BF16)], axis=0)
    row_i = lax.broadcasted_iota(jnp.int32, (WINDOW, 2 * WINDOW), 0)
    col_j = lax.broadcasted_iota(jnp.int32, (WINDOW, 2 * WINDOW), 1)
    band = (col_j > row_i) & (col_j <= row_i + WINDOW)
    band_first = band & ((col_j >= WINDOW) | (s > 0))
    o_blocks = []
    for blk in range(ts // WINDOW):
        qs = _stack_lane_halves(q[blk * WINDOW:(blk + 1) * WINDOW], GROUP).astype(BF16)
        kband = k_ext[blk * WINDOW:blk * WINDOW + 2 * WINDOW]
        vband = v_ext[blk * WINDOW:blk * WINDOW + 2 * WINDOW]
        sc = _dot_nt(qs, kband)
        mask = band_first if blk == 0 else band
        ps = []
        for hh in range(N_HEADS):
            sink_h = sink_ref[_stack_head(hh)]
            sch = jnp.where(mask, sc[hh * WINDOW:(hh + 1) * WINDOW], NEG_INF)
            m = jnp.maximum(jnp.max(sch, axis=-1, keepdims=True), sink_h)
            e = jnp.exp(sch - m)
            den = jnp.sum(e, axis=-1, keepdims=True) + jnp.exp(sink_h - m)
            ps.append((e / den).astype(BF16))
        r = _dot(jnp.concatenate(ps, axis=0), vband)
        o_blocks.append(_unstack_lane_halves(r, WINDOW, GROUP))
    o_a = jnp.concatenate(o_blocks, axis=0)
    kprev[...] = k_ext[ts:ts + WINDOW]
    vprev[...] = v_ext[ts:ts + WINDOW]

    u = g_c * hc
    rows = lax.broadcasted_iota(jnp.int32, (ts, CONV_DIM), 0)
    up1 = uprev[7:8, :]
    up2 = uprev[6:7, :]
    u_m1 = jnp.where(rows == 0, up1, pltpu.roll(u, 1, 0))
    u_m2 = jnp.where(rows == 0, up2, jnp.where(rows == 1, up1, pltpu.roll(u, 2, 0)))
    cw = convw_ref[...]
    conv = cw[0:1, :] * u_m2 + cw[1:2, :] * u_m1 + cw[2:3, :] * u
    o_c = g_b * conv
    uprev[...] = u[ts - 8:ts]

    qms = _stack_lane_halves(qm, MEM_HEADS // 2).astype(BF16)
    o_cols = []
    lo = lax.broadcasted_iota(jnp.int32, (ts, LANES), 1) < HEAD_DIM
    for jb in range(MEM_HEADS // 2):
        scm = _dot_nt(qms[2 * jb * ts:(2 * jb + 2) * ts], mk_s[:, jb * LANES:(jb + 1) * LANES])
        mm = jnp.max(scm, axis=-1, keepdims=True)
        em = jnp.exp(scm - mm)
        pm = (em / jnp.sum(em, axis=-1, keepdims=True)).astype(BF16)
        rm = _dot(pm, mv_s[:, jb * LANES:(jb + 1) * LANES])
        o_cols.append(jnp.where(lo, rm[0:ts], rm[ts:2 * ts]))
    o_m = jnp.concatenate(o_cols, axis=1)

    zg = _dot(xb, w_in_ref[:, _GA0:IN_DIM])
    x1_ref[0] = _merge_project_norm(x, zg, o_a, o_c, o_m, p_attn_ref, p_conv_ref, p_mem_ref, w_o_ref,
                                    g_ref, b_ref, alpha)

    @pl.when(s == ns - 1)
    def _():
        wk_ref[0] = k[ts - WINDOW:ts].T
        wv_ref[0] = v[ts - WINDOW:ts].T
        cv_ref[0] = u[ts - (CONV_WIDTH - 1):ts]


def _mixer_prompt(x, mem, w, l, *, ts, alpha):
    b, seq, _ = x.shape
    kern = functools.partial(_mixer_prompt_kernel, ts=ts, alpha=alpha)
    smem = pl.BlockSpec(memory_space=pltpu.SMEM)
    in_specs = [
        smem,
        pl.BlockSpec((1, ts, D_MODEL), lambda i, s: (i, s, 0)),
        pl.BlockSpec((1, N_MEM, D_MODEL), lambda i, s: (i, 0, 0)),
        _const_spec((D_MODEL, IN_DIM)),
        _const_spec((CONV_WIDTH, CONV_DIM)),
        _const_spec((ATTN_DIM, D_MODEL)),
        _const_spec((CONV_DIM, D_MODEL)),
        _const_spec((MEM_DIM, D_MODEL)),
        _const_spec((D_MODEL, D_MODEL)),
        _const_spec((D_MODEL, MEM_DIM)),
        _const_spec((D_MODEL, MEM_DIM)),
        _const_spec((1, D_MODEL)),
        _const_spec((1, D_MODEL)),
    ]
    out_shape = (
        jax.ShapeDtypeStruct((b, seq, D_MODEL), F32),
        jax.ShapeDtypeStruct((b, KV_DIM, WINDOW), F32),
        jax.ShapeDtypeStruct((b, KV_DIM, WINDOW), F32),
        jax.ShapeDtypeStruct((b, CONV_WIDTH - 1, CONV_DIM), F32),
        jax.ShapeDtypeStruct((b, MEM_DIM, N_MEM), F32),
        jax.ShapeDtypeStruct((b, MEM_DIM, N_MEM), F32),
    )
    out_specs = (
        pl.BlockSpec((1, ts, D_MODEL), lambda i, s: (i, s, 0)),
        pl.BlockSpec((1, KV_DIM, WINDOW), lambda i, s: (i, 0, 0)),
        pl.BlockSpec((1, KV_DIM, WINDOW), lambda i, s: (i, 0, 0)),
        pl.BlockSpec((1, CONV_WIDTH - 1, CONV_DIM), lambda i, s: (i, 0, 0)),
        pl.BlockSpec((1, MEM_DIM, N_MEM), lambda i, s: (i, 0, 0)),
        pl.BlockSpec((1, MEM_DIM, N_MEM), lambda i, s: (i, 0, 0)),
    )
    scratch = [
        pltpu.VMEM((WINDOW, KV_DIM), BF16),
        pltpu.VMEM((WINDOW, KV_DIM), BF16),
        pltpu.VMEM((8, CONV_DIM), F32),
        pltpu.VMEM((N_MEM, MEM_DIM), BF16),
        pltpu.VMEM((N_MEM, MEM_DIM), BF16),
    ]
    return pl.pallas_call(
        kern,
        grid=(b, seq // ts),
        in_specs=in_specs,
        out_specs=out_specs,
        out_shape=out_shape,
        scratch_shapes=scratch,
        compiler_params=pltpu.CompilerParams(dimension_semantics=("arbitrary", "arbitrary"),
                                             vmem_limit_bytes=VMEM_LIMIT),
        name=f"mixer_prompt_l{l}",
    )(w["sink"], x, mem, w["w_in"], w["conv_w"], w["p_attn"], w["p_conv"], w["p_mem"], w["w_o"],
      w["w_mem_k"], w["w_mem_v"], w["ln1_g"], w["ln1_b"])


def _mixer_sample_kernel(sink_ref, x_ref, ck_ref, cvv_ref, cc_ref, cmk_ref, cmv_ref, w_in_ref, wkvt_ref, convw_ref,
                         p_attn_ref, p_conv_ref, p_mem_ref, w_o_ref, g_ref, b_ref,
                         x1_ref, wk_ref, wv_ref, cv_ref, *, bb, t, alpha):
    nbuf = ck_ref.shape[2]
    x = x_ref[...]
    xb = x.astype(BF16)
    z = _dot(xb, w_in_ref[:, 0:_MIX_COLS])
    q = z[:, _Q0:_K0] * ATTN_SCALE
    u = z[:, _GC0:_QM0] * z[:, _H0:_GB0]
    g_b = z[:, _GB0:_GC0]
    qm = z[:, _QM0:_GA0] * ATTN_SCALE
    cw = convw_ref[...]
    kvt = _dot_nt(wkvt_ref[...], xb)
    kt_new = kvt[0:KV_DIM]
    vt_new = kvt[KV_DIM:2 * KV_DIM]

    nstk = N_HEADS * t
    r_t = lax.broadcasted_iota(jnp.int32, (nstk, 2 * WINDOW), 0) % t
    c_j = lax.broadcasted_iota(jnp.int32, (nstk, 2 * WINDOW), 1)
    dist = r_t + nbuf - c_j
    wmask = (dist >= 0) & (dist < WINDOW) & (c_j < nbuf + t)
    blk_id = lax.broadcasted_iota(jnp.int32, (nstk, 1), 0) // t
    sink_col = jnp.zeros((nstk, 1), F32)
    for hh in range(N_HEADS):
        sink_col = jnp.where(blk_id == hh, sink_ref[_stack_head(hh)], sink_col)
    mlane = lax.broadcasted_iota(jnp.int32, (t, MEM_DIM), 1) // HEAD_DIM
    klane = lax.broadcasted_iota(jnp.int32, (KV_DIM, LANES), 1)

    o_a_rows, o_c_rows, o_m_rows = [], [], []
    for bi in range(bb):
        sl = slice(bi * t, (bi + 1) * t)
        kc = ck_ref[bi]
        vc = cvv_ref[bi]
        k_new0 = jnp.where(klane < t, pltpu.roll(kt_new, (LANES - bi * t) % LANES, 1), 0.0)
        v_new0 = jnp.where(klane < t, pltpu.roll(vt_new, (LANES - bi * t) % LANES, 1), 0.0)
        k_all = jnp.concatenate([kc, k_new0], axis=1).astype(BF16)
        v_all = jnp.concatenate([vc, v_new0], axis=1).astype(BF16)
        qs = _stack_lane_halves(q[sl], GROUP).astype(BF16)
        sc = jnp.where(wmask, _dot(qs, k_all), NEG_INF)
        m = jnp.maximum(jnp.max(sc, axis=-1, keepdims=True), sink_col)
        e = jnp.exp(sc - m)
        den = jnp.sum(e, axis=-1, keepdims=True) + jnp.exp(sink_col - m)
        r = _dot_nt((e / den).astype(BF16), v_all)
        o_a_rows.append(_unstack_lane_halves(r, t, GROUP))
        tail = (nbuf - t - bi * t) % LANES
        wk_ref[bi] = jnp.where(klane < nbuf - t, pltpu.roll(kc, nbuf - t, 1), pltpu.roll(kt_new, tail, 1))
        wv_ref[bi] = jnp.where(klane < nbuf - t, pltpu.roll(vc, nbuf - t, 1), pltpu.roll(vt_new, tail, 1))
        u_ext = jnp.concatenate([cc_ref[bi], u[sl]], axis=0)
        conv = cw[0:1, :] * u_ext[0:t] + cw[1:2, :] * u_ext[1:t + 1] + cw[2:3, :] * u_ext[2:t + 2]
        o_c_rows.append(g_b[sl] * conv)
        cv_ref[bi] = u_ext[t:t + CONV_WIDTH - 1]
        qmb = qm[sl]
        qms = jnp.concatenate([jnp.where(mlane == hh, qmb, 0.0) for hh in range(MEM_HEADS)], axis=0).astype(BF16)
        scm = _dot(qms, cmk_ref[bi].astype(BF16))
        mm = jnp.max(scm, axis=-1, keepdims=True)
        em = jnp.exp(scm - mm)
        pm = (em / jnp.sum(em, axis=-1, keepdims=True)).astype(BF16)
        rm = _dot_nt(pm, cmv_ref[bi].astype(BF16))
        om = jnp.zeros((t, MEM_DIM), F32)
        for hh in range(MEM_HEADS):
            om = jnp.where(mlane == hh, rm[hh * t:(hh + 1) * t], om)
        o_m_rows.append(om)

    o_a = jnp.concatenate(o_a_rows, axis=0)
    o_c = jnp.concatenate(o_c_rows, axis=0)
    o_m = jnp.concatenate(o_m_rows, axis=0)
    zg = _dot(xb, w_in_ref[:, _GA0:IN_DIM])
    x1_ref[...] = _merge_project_norm(x, zg, o_a, o_c, o_m, p_attn_ref, p_conv_ref, p_mem_ref, w_o_ref,
                                      g_ref, b_ref, alpha)


def _mixer_sample(x2d, ckt, cvt, cc, cmkt, cmvt, w, l, *, bb, t, alpha):
    nb, nbuf = ckt.shape[1], ckt.shape[3]
    assert nbuf == LANES and bb * t == LANES and t % 8 == 0
    kern = functools.partial(_mixer_sample_kernel, bb=bb, t=t, alpha=alpha)
    smem = pl.BlockSpec(memory_space=pltpu.SMEM)

    def cache_spec(d1, d2):
        return pl.BlockSpec((None, bb, d1, d2), lambda i: (l, i, 0, 0))

    in_specs = [
        smem,
        pl.BlockSpec((bb * t, D_MODEL), lambda i: (i, 0)),
        cache_spec(KV_DIM, nbuf),
        cache_spec(KV_DIM, nbuf),
        cache_spec(CONV_WIDTH - 1, CONV_DIM),
        cache_spec(MEM_DIM, N_MEM),
        cache_spec(MEM_DIM, N_MEM),
        _const_spec((D_MODEL, IN_DIM)),
        _const_spec((2 * KV_DIM, D_MODEL)),
        _const_spec((CONV_WIDTH, CONV_DIM)),
        _const_spec((ATTN_DIM, D_MODEL)),
        _const_spec((CONV_DIM, D_MODEL)),
        _const_spec((MEM_DIM, D_MODEL)),
        _const_spec((D_MODEL, D_MODEL)),
        _const_spec((1, D_MODEL)),
        _const_spec((1, D_MODEL)),
    ]
    out_shape = (
        jax.ShapeDtypeStruct((nb * t, D_MODEL), F32),
        jax.ShapeDtypeStruct((nb, KV_DIM, nbuf), F32),
        jax.ShapeDtypeStruct((nb, KV_DIM, nbuf), F32),
        jax.ShapeDtypeStruct((nb, CONV_WIDTH - 1, CONV_DIM), F32),
    )
    out_specs = (
        pl.BlockSpec((bb * t, D_MODEL), lambda i: (i, 0)),
        pl.BlockSpec((bb, KV_DIM, nbuf), lambda i: (i, 0, 0)),
        pl.BlockSpec((bb, KV_DIM, nbuf), lambda i: (i, 0, 0)),
        pl.BlockSpec((bb, CONV_WIDTH - 1, CONV_DIM), lambda i: (i, 0, 0)),
    )
    return pl.pallas_call(
        kern,
        grid=(nb // bb,),
        in_specs=in_specs,
        out_specs=out_specs,
        out_shape=out_shape,
        compiler_params=pltpu.CompilerParams(dimension_semantics=("arbitrary",),
                                             vmem_limit_bytes=VMEM_LIMIT),
        name=f"mixer_sample_l{l}",
    )(w["sink"], x2d, ckt, cvt, cc, cmkt, cmvt, w["w_in"], w["w_kvt"], w["conv_w"], w["p_attn"], w["p_conv"],
      w["p_mem"], w["w_o"], w["ln1_g"], w["ln1_b"])


def _swiglu(xb, wg, wu, wd):
    g = _dot(xb, wg)
    u = _dot(xb, wu)
    return _dot((jax.nn.silu(g) * u).astype(BF16), wd)


def _ffn_kernel(x_ref, wg_ref, wu_ref, wd_ref, g_ref, b_ref, o_ref, *, alpha):
    x = x_ref[...]
    y = _swiglu(x.astype(BF16), wg_ref[...], wu_ref[...], wd_ref[...])
    o_ref[...] = _layer_norm(alpha * x + y, g_ref[...], b_ref[...])


def _ffn_dense(x, wg, wu, wd, g, b, *, tm, alpha, name):
    n = x.shape[0]
    return pl.pallas_call(
        functools.partial(_ffn_kernel, alpha=alpha),
        grid=(n // tm,),
        in_specs=[pl.BlockSpec((tm, D_MODEL), lambda i: (i, 0)),
                  _const_spec((D_MODEL, D_FF)), _const_spec((D_MODEL, D_FF)), _const_spec((D_FF, D_MODEL)),
                  _const_spec((1, D_MODEL)), _const_spec((1, D_MODEL))],
        out_specs=pl.BlockSpec((tm, D_MODEL), lambda i: (i, 0)),
        out_shape=jax.ShapeDtypeStruct((n, D_MODEL), F32),
        compiler_params=pltpu.CompilerParams(dimension_semantics=("arbitrary",), vmem_limit_bytes=VMEM_LIMIT),
        name=name,
    )(x, wg, wu, wd, g, b)


def _two_group_specs(tm, n_a_tiles, n_b_tiles):
    spec_a = pl.BlockSpec((tm, D_MODEL), lambda i: (jnp.minimum(i, n_a_tiles - 1), 0))
    spec_b = pl.BlockSpec((tm, D_MODEL), lambda i: (jnp.clip(i - n_a_tiles, 0, n_b_tiles - 1), 0))
    return spec_a, spec_b


def _route_kernel(xa_ref, xb_ref, wrt_ref, rb_ref, tri_ref, slab_ref, cnt_ref, carry, *, tm, n_a_tiles):
    i = pl.program_id(0)

    @pl.when(i == 0)
    def _():
        carry[...] = jnp.zeros_like(carry)

    x = jnp.where(i < n_a_tiles, xa_ref[...], xb_ref[...])
    lg = _dot_nt(wrt_ref[...], x.astype(BF16)) + rb_ref[...]
    e_iota = lax.broadcasted_iota(jnp.int32, (N_EXPERTS, tm), 0)
    m1 = jnp.max(lg, axis=0, keepdims=True)
    i1 = jnp.min(jnp.where(lg == m1, e_iota, N_EXPERTS), axis=0, keepdims=True)
    lg2 = jnp.where(e_iota == i1, NEG_INF, lg)
    m2 = jnp.max(lg2, axis=0, keepdims=True)
    i2 = jnp.min(jnp.where(lg2 == m2, e_iota, N_EXPERTS), axis=0, keepdims=True)
    e2 = jnp.exp(m2 - m1)
    den = 1.0 + e2
    w1 = 1.0 / den
    w2 = e2 / den
    sel1 = e_iota == i1
    sel2 = e_iota == i2
    oh = jnp.where(sel1 | sel2, 1.0, 0.0)
    pre = _dot(oh.astype(BF16), tri_ref[...]) + carry[:, 0:1]
    r1 = jnp.sum(jnp.where(sel1, pre, 0.0), axis=0, keepdims=True)
    r2 = jnp.sum(jnp.where(sel2, pre, 0.0), axis=0, keepdims=True)
    carry[...] = carry[...] + jnp.sum(oh, axis=1, keepdims=True)
    rows = [i1.astype(F32), i2.astype(F32), r1, r2, w1, w2]
    slab = jnp.zeros((N_EXPERTS, tm), F32)
    for ri, rv in enumerate(rows):
        slab = jnp.where(e_iota == ri, rv, slab)
    slab_ref[...] = slab
    cnt_ref[...] = carry[...]


def _route(xa, xb, wrt, rb, *, tm):
    na, nb_ = xa.shape[0] // tm, xb.shape[0] // tm
    tri = jnp.triu(jnp.ones((tm, tm), BF16), k=1)
    spec_a, spec_b = _two_group_specs(tm, na, nb_)
    return pl.pallas_call(
        functools.partial(_route_kernel, tm=tm, n_a_tiles=na),
        grid=(na + nb_,),
        in_specs=[spec_a, spec_b,
                  _const_spec((N_EXPERTS, D_MODEL)), _const_spec((N_EXPERTS, 1)), _const_spec((tm, tm))],
        out_specs=(pl.BlockSpec((N_EXPERTS, tm), lambda i: (0, i)),
                   pl.BlockSpec((N_EXPERTS, LANES), lambda i: (0, 0))),
        out_shape=(jax.ShapeDtypeStruct((N_EXPERTS, (na + nb_) * tm), F32),
                   jax.ShapeDtypeStruct((N_EXPERTS, LANES), F32)),
        scratch_shapes=[pltpu.VMEM((N_EXPERTS, LANES), F32)],
        compiler_params=pltpu.CompilerParams(dimension_semantics=("arbitrary",)),
        name="moe_route",
    )(xa, xb, wrt, rb, tri)


def _row_copy(src, src_row, dst, dst_row, sem):
    return pltpu.make_async_copy(src.at[pl.ds(src_row, 1)], dst.at[pl.ds(dst_row, 1)], sem)


def _dispatch_kernel(slot_ref, xa_ref, xb_ref, z_ref, xs_ref, sem, *, tm, n_a_tiles, n_b_tiles):
    i = pl.program_id(0)

    def issue_from(src_ref):
        def issue(r, c):
            _row_copy(src_ref, r, xs_ref, slot_ref[0, 0, 2 * r], sem).start(priority=0)
            _row_copy(src_ref, r, xs_ref, slot_ref[0, 0, 2 * r + 1], sem).start(priority=1)
            return c

        lax.fori_loop(0, tm, issue, 0, unroll=8)

    @pl.when(i < n_a_tiles)
    def _():
        issue_from(xa_ref)

    @pl.when((i >= n_a_tiles) & (i < n_a_tiles + n_b_tiles))
    def _():
        issue_from(xb_ref)

    @pl.when(i >= n_a_tiles + n_b_tiles)
    def _():
        issue_from(z_ref)

    for _ in range(2):
        pltpu.make_async_copy(z_ref, xs_ref.at[pl.ds(0, tm)], sem).wait()


def _dispatch(xa, xb, slots3, n_rows, *, tm):
    na, nb_ = xa.shape[0] // tm, xb.shape[0] // tm
    nz = slots3.shape[0] - na - nb_
    spec_a, spec_b = _two_group_specs(tm, na, nb_)
    zsrc = jnp.zeros((tm, D_MODEL), F32)
    return pl.pallas_call(
        functools.partial(_dispatch_kernel, tm=tm, n_a_tiles=na, n_b_tiles=nb_),
        grid=(na + nb_ + nz,),
        in_specs=[pl.BlockSpec((1, 1, 2 * tm), lambda i: (i, 0, 0), memory_space=pltpu.SMEM),
                  spec_a, spec_b, _const_spec((tm, D_MODEL))],
        out_specs=pl.BlockSpec(memory_space=pl.ANY),
        out_shape=jax.ShapeDtypeStruct((n_rows, D_MODEL), F32),
        scratch_shapes=[pltpu.SemaphoreType.DMA(())],
        compiler_params=pltpu.CompilerParams(dimension_semantics=("arbitrary",)),
        name="moe_dispatch",
    )(slots3, xa, xb, zsrc)


def _expert_kernel(te_ref, nt_ref, xs_ref, wg_ref, wu_ref, wd_ref, o_ref):
    del te_ref
    live = pl.program_id(0) < nt_ref[0]

    @pl.when(live)
    def _():
        o_ref[...] = _swiglu(xs_ref[...].astype(BF16), wg_ref[0], wu_ref[0], wd_ref[0])

    @pl.when(jnp.logical_not(live))
    def _():
        o_ref[...] = jnp.zeros_like(o_ref)


def _experts(xs, tile_expert, n_tiles, wg, wu, wd, *, tm):
    n_rows = xs.shape[0]

    def row_map(i, te, nt):
        return (jnp.minimum(i, nt[0] - 1), 0)

    def w_map(i, te, nt):
        return (te[i], 0, 0)

    return pl.pallas_call(
        _expert_kernel,
        grid_spec=pltpu.PrefetchScalarGridSpec(
            num_scalar_prefetch=2,
            grid=(n_rows // tm,),
            in_specs=[pl.BlockSpec((tm, D_MODEL), row_map),
                      pl.BlockSpec((1, D_MODEL, D_FF), w_map),
                      pl.BlockSpec((1, D_MODEL, D_FF), w_map),
                      pl.BlockSpec((1, D_FF, D_MODEL), w_map)],
            out_specs=pl.BlockSpec((tm, D_MODEL), lambda i, te, nt: (i, 0)),
        ),
        out_shape=jax.ShapeDtypeStruct((n_rows, D_MODEL), F32),
        compiler_params=pltpu.CompilerParams(dimension_semantics=("arbitrary",), vmem_limit_bytes=VMEM_LIMIT),
        name="moe_experts",
    )(tile_expert, n_tiles, xs, wg, wu, wd)


def _combine_kernel(slot_ref, x_ref, gate_ref, ys_ref, g_ref, b_ref, o_ref, buf0, buf1, sem, *, tm, alpha):
    def issue(r, c):
        _row_copy(ys_ref, slot_ref[0, 0, 2 * r], buf0, r, sem).start(priority=0)
        _row_copy(ys_ref, slot_ref[0, 0, 2 * r + 1], buf1, r, sem).start(priority=1)
        return c

    lax.fori_loop(0, tm, issue, 0, unroll=8)

    pltpu.make_async_copy(ys_ref.at[pl.ds(0, tm)], buf0, sem).wait()
    pltpu.make_async_copy(ys_ref.at[pl.ds(0, tm)], buf1, sem).wait()
    gates = gate_ref[...]
    y = gates[:, 0:1] * buf0[...] + gates[:, 1:2] * buf1[...]
    o_ref[...] = _layer_norm(alpha * x_ref[...] + y, g_ref[...], b_ref[...])


def _combine(x, slots3, gates, ys, g, b, *, tm, alpha, name):
    n = x.shape[0]
    return pl.pallas_call(
        functools.partial(_combine_kernel, tm=tm, alpha=alpha),
        grid=(n // tm,),
        in_specs=[pl.BlockSpec((1, 1, 2 * tm), lambda i: (i, 0, 0), memory_space=pltpu.SMEM),
                  pl.BlockSpec((tm, D_MODEL), lambda i: (i, 0)),
                  pl.BlockSpec((tm, 2), lambda i: (i, 0)),
                  pl.BlockSpec(memory_space=pl.ANY),
                  _const_spec((1, D_MODEL)), _const_spec((1, D_MODEL))],
        out_specs=pl.BlockSpec((tm, D_MODEL), lambda i: (i, 0)),
        out_shape=jax.ShapeDtypeStruct((n, D_MODEL), F32),
        scratch_shapes=[pltpu.VMEM((tm, D_MODEL), F32), pltpu.VMEM((tm, D_MODEL), F32),
                        pltpu.SemaphoreType.DMA(())],
        compiler_params=pltpu.CompilerParams(dimension_semantics=("arbitrary",)),
        name=name,
    )(slots3, x, gates, ys, g, b)


def _free_slots(counts, offs, padded, n_rows, n_free):
    starts = jnp.concatenate([offs + counts, (offs[-1:] + padded[-1:])])
    gaps = jnp.concatenate([padded - counts, n_rows - (offs[-1:] + padded[-1:])])
    cum = jnp.cumsum(gaps)
    prev = cum - gaps
    idx = jnp.arange(n_free, dtype=jnp.int32)
    seg = jnp.sum(idx[:, None] >= cum[None, :], axis=1)
    pick = seg[:, None] == jnp.arange(N_EXPERTS + 1, dtype=jnp.int32)[None, :]
    return jnp.sum(jnp.where(pick, (starts - prev)[None, :], 0), axis=1) + idx


def _moe(xa, xb, router_w, router_b, wg, wu, wd, g, b, *, alpha, tm_rows, tm_expert):
    na, nb_ = xa.shape[0], xb.shape[0]
    n = na + nb_
    slab, cnt = _route(xa, xb, router_w.T.astype(BF16), router_b.reshape(N_EXPERTS, 1), tm=tm_rows)
    counts = cnt[:, 0].astype(jnp.int32)
    tiles_per = (counts + tm_expert - 1) // tm_expert
    padded = tiles_per * tm_expert
    ends = jnp.cumsum(tiles_per)
    offs = (ends - tiles_per) * tm_expert
    i1 = slab[0].astype(jnp.int32)
    i2 = slab[1].astype(jnp.int32)
    eye = jnp.arange(N_EXPERTS, dtype=jnp.int32)[:, None]
    off1 = jnp.sum(jnp.where(eye == i1[None, :], offs[:, None], 0), axis=0)
    off2 = jnp.sum(jnp.where(eye == i2[None, :], offs[:, None], 0), axis=0)
    slots = jnp.stack([off1 + slab[2].astype(jnp.int32), off2 + slab[3].astype(jnp.int32)], axis=1)
    gates = jnp.stack([slab[4], slab[5]], axis=1)
    n_free = N_EXPERTS * tm_expert
    n_rows = 2 * n + n_free
    assert n_free % (2 * tm_rows) == 0
    free = _free_slots(counts, offs, padded, n_rows, n_free)
    max_tiles = n_rows // tm_expert
    tile_expert = jnp.minimum(
        jnp.sum(jnp.arange(max_tiles, dtype=jnp.int32)[:, None] >= ends[None, :], axis=1), N_EXPERTS - 1
    ).astype(jnp.int32)
    n_tiles = ends[-1:].astype(jnp.int32)
    slots3 = jnp.concatenate([slots.reshape(-1), free]).reshape(-1, 1, 2 * tm_rows)

    xs = _dispatch(xa, xb, slots3, n_rows, tm=tm_rows)
    ys = _experts(xs, tile_expert, n_tiles, wg, wu, wd, tm=tm_expert)
    ta = na // tm_rows
    ya = _combine(xa, slots3[:ta], gates[:na], ys, g, b, tm=tm_rows, alpha=alpha, name="moe_combine_a")
    yb = _combine(xb, slots3[ta:ta + nb_ // tm_rows], gates[na:], ys, g, b, tm=tm_rows, alpha=alpha,
                  name="moe_combine_b")
    return ya, yb


def _layer_weights(l, w_in, conv_w, sink, p_attn, p_conv, p_mem, w_o, w_mem_k, w_mem_v, ln1_g, ln1_b):
    w_in_l = w_in[l].astype(BF16)
    w_q = w_in_l[:, :ATTN_DIM].reshape(D_MODEL, N_KV_HEADS, GROUP, HEAD_DIM)
    w_q = jnp.transpose(w_q, (0, 2, 1, 3)).reshape(D_MODEL, ATTN_DIM)
    p_a = p_attn[l].astype(BF16).reshape(N_KV_HEADS, GROUP, HEAD_DIM, D_MODEL)
    p_a = jnp.transpose(p_a, (1, 0, 2, 3)).reshape(ATTN_DIM, D_MODEL)
    return dict(
        sink=sink[l],
        w_in=jnp.concatenate([w_q, w_in_l[:, ATTN_DIM:]], axis=1),
        w_kvt=w_in_l[:, _K0:_H0].T,
        conv_w=conv_w[l],
        p_attn=p_a,
        p_conv=p_conv[l].astype(BF16),
        p_mem=p_mem[l].astype(BF16),
        w_o=w_o[l].astype(BF16),
        w_mem_k=w_mem_k[l].astype(BF16),
        w_mem_v=w_mem_v[l].astype(BF16),
        ln1_g=ln1_g[l].reshape(1, D_MODEL),
        ln1_b=ln1_b[l].reshape(1, D_MODEL),
    )


def _feature_major(c):
    d, nb, tok, h, hd = c.shape
    return jnp.transpose(c, (0, 1, 3, 4, 2)).reshape(d, nb, h * hd, tok)


def _token_major(c, heads):
    d, nb, f, tok = c.shape
    return jnp.transpose(c.reshape(d, nb, heads, f // heads, tok), (0, 1, 4, 2, 3))


def kernel(x_prompt, x_sample, mem_prompt, cache_win_k, cache_win_v, cache_conv, cache_mem_k, cache_mem_v, w_in, conv_w, sink, p_attn, p_conv, p_mem, w_o, w_mem_k, w_mem_v, ln1_g, ln1_b, ln2_g, ln2_b, ffn_w_gate, ffn_w_up, ffn_w_down, router_w, router_b, exp_w_gate, exp_w_up, exp_w_down):
    depth = w_in.shape[0]
    alpha = (2 * depth) ** 0.25
    b, seq, _ = x_prompt.shape
    nb, t, _ = x_sample.shape
    n_p = b * seq
    n_s = nb * t

    xp = x_prompt
    xs = x_sample.reshape(n_s, D_MODEL)
    ckt = _feature_major(cache_win_k)
    cvt = _feature_major(cache_win_v)
    cmkt = _feature_major(cache_mem_k)
    cmvt = _feature_major(cache_mem_v)

    outs = {k_: [] for k_ in ("wk_p", "wv_p", "cv_p", "mk_p", "mv_p", "wk_s", "wv_s", "cv_s")}
    for l in range(depth):
        w = _layer_weights(l, w_in, conv_w, sink, p_attn, p_conv, p_mem, w_o, w_mem_k, w_mem_v, ln1_g, ln1_b)
        x1p, wk, wv, cvp, mk, mv = _mixer_prompt(xp, mem_prompt, w, l, ts=512, alpha=alpha)
        x1s, wks, wvs, cvs = _mixer_sample(xs, ckt, cvt, cache_conv, cmkt, cmvt, w, l, bb=LANES // t, t=t, alpha=alpha)
        outs["wk_p"].append(wk); outs["wv_p"].append(wv); outs["cv_p"].append(cvp)
        outs["mk_p"].append(mk); outs["mv_p"].append(mv)
        outs["wk_s"].append(wks); outs["wv_s"].append(wvs); outs["cv_s"].append(cvs)
        x1p = x1p.reshape(n_p, D_MODEL)
        g2 = ln2_g[l].reshape(1, D_MODEL)
        b2 = ln2_b[l].reshape(1, D_MODEL)
        i = l // 2
        if l % 2 == 0:
            wg, wu, wd = ffn_w_gate[i].astype(BF16), ffn_w_up[i].astype(BF16), ffn_w_down[i].astype(BF16)
            x2p = _ffn_dense(x1p, wg, wu, wd, g2, b2, tm=512, alpha=alpha, name=f"ffn_dense_p_l{l}")
            xs = _ffn_dense(x1s, wg, wu, wd, g2, b2, tm=512, alpha=alpha, name=f"ffn_dense_s_l{l}")
        else:
            x2p, xs = _moe(x1p, x1s, router_w[i], router_b[i], exp_w_gate[i].astype(BF16), exp_w_up[i].astype(BF16),
                           exp_w_down[i].astype(BF16), g2, b2, alpha=alpha, tm_rows=512, tm_expert=256)
        xp = x2p.reshape(b, seq, D_MODEL)

    return (xp, xs.reshape(nb, t, D_MODEL),
            _token_major(jnp.stack(outs["wk_p"]), N_KV_HEADS),
            _token_major(jnp.stack(outs["wv_p"]), N_KV_HEADS),
            jnp.stack(outs["cv_p"]),
            _token_major(jnp.stack(outs["mk_p"]), MEM_HEADS),
            _token_major(jnp.stack(outs["mv_p"]), MEM_HEADS),
            _token_major(jnp.stack(outs["wk_s"]), N_KV_HEADS),
            _token_major(jnp.stack(outs["wv_s"]), N_KV_HEADS),
            jnp.stack(outs["cv_s"]))
```

```python
import functools

import jax
import jax.numpy as jnp
from jax import lax
from jax.experimental import pallas as pl
from jax.experimental.pallas import tpu as pltpu

F32 = jnp.float32
BF16 = jnp.bfloat16

D_MODEL = 1024
HEAD_DIM = 64
N_HEADS = 8
N_KV_HEADS = 2
GROUP = N_HEADS // N_KV_HEADS
ATTN_DIM = N_HEADS * HEAD_DIM
KV_DIM = N_KV_HEADS * HEAD_DIM
WINDOW = 128
ATTN_SCALE = HEAD_DIM ** -0.5
CONV_DIM = 256
CONV_WIDTH = 3
N_MEM = 256
MEM_HEADS = 4
MEM_DIM = MEM_HEADS * HEAD_DIM
D_FF = 2816
N_EXPERTS = 8
LN_EPS = 1e-5

_Q0, _K0, _V0, _H0, _GB0, _GC0, _QM0, _GA0 = 0, 512, 640, 768, 1024, 1280, 1536, 1792
_MIX_COLS = _GA0
IN_DIM = _GA0 + 3 * D_MODEL

LANES = 128
VMEM_LIMIT = 56 * 1024 * 1024

NEG_INF = float("-inf")


def _stack_head(s):
    return (s // 2) if s % 2 == 0 else GROUP + s // 2


def _layer_norm(v, g, b):
    mu = jnp.mean(v, axis=-1, keepdims=True)
    c = v - mu
    var = jnp.mean(c * c, axis=-1, keepdims=True)
    return c * lax.rsqrt(var + LN_EPS) * g + b


def _dot(a, b):
    return jnp.dot(a, b, preferred_element_type=F32)


def _dot_nt(a, b):
    return lax.dot_general(a, b, (((1,), (1,)), ((), ())), preferred_element_type=F32)


def _stack_lane_halves(q, n_blocks):
    m = q.shape[0]
    lo = lax.broadcasted_iota(jnp.int32, (m, LANES), 1) < HEAD_DIM
    parts = []
    for j in range(n_blocks):
        blk = q[:, j * LANES:(j + 1) * LANES]
        parts.append(jnp.where(lo, blk, 0.0))
        parts.append(jnp.where(lo, 0.0, blk))
    return jnp.concatenate(parts, axis=0)


def _unstack_lane_halves(r, m, n_blocks):
    lo = lax.broadcasted_iota(jnp.int32, (m, LANES), 1) < HEAD_DIM
    cols = [jnp.where(lo, r[(2 * j) * m:(2 * j + 1) * m], r[(2 * j + 1) * m:(2 * j + 2) * m])
            for j in range(n_blocks)]
    return jnp.concatenate(cols, axis=1)


def _merge_project_norm(x, zg, o_a, o_c, o_m, p_attn_ref, p_conv_ref, p_mem_ref, w_o_ref, g_ref, b_ref, alpha):
    merged = (jax.nn.sigmoid(zg[:, 0:D_MODEL]) * _dot(o_a.astype(BF16), p_attn_ref[...])
              + jax.nn.sigmoid(zg[:, D_MODEL:2 * D_MODEL]) * _dot(o_c.astype(BF16), p_conv_ref[...])
              + jax.nn.sigmoid(zg[:, 2 * D_MODEL:3 * D_MODEL]) * _dot(o_m.astype(BF16), p_mem_ref[...]))
    mix = _dot(merged.astype(BF16), w_o_ref[...])
    return _layer_norm(alpha * x + mix, g_ref[...], b_ref[...])


def _const_spec(shape):
    n = len(shape)
    return pl.BlockSpec(shape, lambda *_: (0,) * n, pipeline_mode=pl.Buffered(1))


def _mixer_prompt_kernel(sink_ref, x_ref, mem_ref, w_in_ref, convw_ref, p_attn_ref, p_conv_ref, p_mem_ref,
                         w_o_ref, wmk_ref, wmv_ref, g_ref, b_ref,
                         x1_ref, wk_ref, wv_ref, cv_ref, mk_ref, mv_ref,
                         kprev, vprev, uprev, mk_s, mv_s, *, ts, alpha):
    s = pl.program_id(1)
    ns = pl.num_programs(1)
    x = x_ref[0]
    xb = x.astype(BF16)

    @pl.when(s == 0)
    def _():
        memb = mem_ref[0].astype(BF16)
        mk = _dot(memb, wmk_ref[...])
        mv = _dot(memb, wmv_ref[...])
        mk_ref[0] = mk.T
        mv_ref[0] = mv.T
        mk_s[...] = mk.astype(BF16)
        mv_s[...] = mv.astype(BF16)
        kprev[...] = jnp.zeros_like(kprev)
        vprev[...] = jnp.zeros_like(vprev)
        uprev[...] = jnp.zeros_like(uprev)

    z = _dot(xb, w_in_ref[:, 0:_MIX_COLS])
    q = z[:, _Q0:_K0] * ATTN_SCALE
    k = z[:, _K0:_V0]
    v = z[:, _V0:_H0]
    hc = z[:, _H0:_GB0]
    g_b = z[:, _GB0:_GC0]
    g_c = z[:, _GC0:_QM0]
    qm = z[:, _QM0:_GA0] * ATTN_SCALE

    k_ext = jnp.concatenate([kprev[...], k.astype(BF16)], axis=0)
    v_ext = jnp.concatenate([vprev[...], v.astype(BF16)], axis=0)
    row_i = lax.broadcasted_iota(jnp.int32, (WINDOW, 2 * WINDOW), 0)
    col_j = lax.broadcasted_iota(jnp.int32, (WINDOW, 2 * WINDOW), 1)
    band = (col_j > row_i) & (col_j <= row_i + WINDOW)
    band_first = band & ((col_j >= WINDOW) | (s > 0))
    o_blocks = []
    for blk in range(ts // WINDOW):
        qs = _stack_lane_halves(q[blk * WINDOW:(blk + 1) * WINDOW], GROUP).astype(BF16)
        kband = k_ext[blk * WINDOW:blk * WINDOW + 2 * WINDOW]
        vband = v_ext[blk * WINDOW:blk * WINDOW + 2 * WINDOW]
        sc = _dot_nt(qs, kband)
        mask = band_first if blk == 0 else band
        ps = []
        for hh in range(N_HEADS):
            sink_h = sink_ref[_stack_head(hh)]
            sch = jnp.where(mask, sc[hh * WINDOW:(hh + 1) * WINDOW], NEG_INF)
            m = jnp.maximum(jnp.max(sch, axis=-1, keepdims=True), sink_h)
            e = jnp.exp(sch - m)
            den = jnp.sum(e, axis=-1, keepdims=True) + jnp.exp(sink_h - m)
            ps.append((e / den).astype(BF16))
        r = _dot(jnp.concatenate(ps, axis=0), vband)
        o_blocks.append(_unstack_lane_halves(r, WINDOW, GROUP))
    o_a = jnp.concatenate(o_blocks, axis=0)
    kprev[...] = k_ext[ts:ts + WINDOW]
    vprev[...] = v_ext[ts:ts + WINDOW]

    u = g_c * hc
    rows = lax.broadcasted_iota(jnp.int32, (ts, CONV_DIM), 0)
    up1 = uprev[7:8, :]
    up2 = uprev[6:7, :]
    u_m1 = jnp.where(rows == 0, up1, pltpu.roll(u, 1, 0))
    u_m2 = jnp.where(rows == 0, up2, jnp.where(rows == 1, up1, pltpu.roll(u, 2, 0)))
    cw = convw_ref[...]
    conv = cw[0:1, :] * u_m2 + cw[1:2, :] * u_m1 + cw[2:3, :] * u
    o_c = g_b * conv
    uprev[...] = u[ts - 8:ts]

    qms = _stack_lane_halves(qm, MEM_HEADS // 2).astype(BF16)
    o_cols = []
    lo = lax.broadcasted_iota(jnp.int32, (ts, LANES), 1) < HEAD_DIM
    for jb in range(MEM_HEADS // 2):
        scm = _dot_nt(qms[2 * jb * ts:(2 * jb + 2) * ts], mk_s[:, jb * LANES:(jb + 1) * LANES])
        mm = jnp.max(scm, axis=-1, keepdims=True)
        em = jnp.exp(scm - mm)
        pm = (em / jnp.sum(em, axis=-1, keepdims=True)).astype(BF16)
        rm = _dot(pm, mv_s[:, jb * LANES:(jb + 1) * LANES])
        o_cols.append(jnp.where(lo, rm[0:ts], rm[ts:2 * ts]))
    o_m = jnp.concatenate(o_cols, axis=1)

    zg = _dot(xb, w_in_ref[:, _GA0:IN_DIM])
    x1_ref[0] = _merge_project_norm(x, zg, o_a, o_c, o_m, p_attn_ref, p_conv_ref, p_mem_ref, w_o_ref,
                                    g_ref, b_ref, alpha)

    @pl.when(s == ns - 1)
    def _():
        wk_ref[0] = k[ts - WINDOW:ts].T
        wv_ref[0] = v[ts - WINDOW:ts].T
        cv_ref[0] = u[ts - (CONV_WIDTH - 1):ts]


def _mixer_prompt(x, mem, w, l, *, ts, alpha):
    b, seq, _ = x.shape
    kern = functools.partial(_mixer_prompt_kernel, ts=ts, alpha=alpha)
    smem = pl.BlockSpec(memory_space=pltpu.SMEM)
    in_specs = [
        smem,
        pl.BlockSpec((1, ts, D_MODEL), lambda i, s: (i, s, 0)),
        pl.BlockSpec((1, N_MEM, D_MODEL), lambda i, s: (i, 0, 0)),
        _const_spec((D_MODEL, IN_DIM)),
        _const_spec((CONV_WIDTH, CONV_DIM)),
        _const_spec((ATTN_DIM, D_MODEL)),
        _const_spec((CONV_DIM, D_MODEL)),
        _const_spec((MEM_DIM, D_MODEL)),
        _const_spec((D_MODEL, D_MODEL)),
        _const_spec((D_MODEL, MEM_DIM)),
        _const_spec((D_MODEL, MEM_DIM)),
        _const_spec((1, D_MODEL)),
        _const_spec((1, D_MODEL)),
    ]
    out_shape = (
        jax.ShapeDtypeStruct((b, seq, D_MODEL), F32),
        jax.ShapeDtypeStruct((b, KV_DIM, WINDOW), F32),
        jax.ShapeDtypeStruct((b, KV_DIM, WINDOW), F32),
        jax.ShapeDtypeStruct((b, CONV_WIDTH - 1, CONV_DIM), F32),
        jax.ShapeDtypeStruct((b, MEM_DIM, N_MEM), F32),
        jax.ShapeDtypeStruct((b, MEM_DIM, N_MEM), F32),
    )
    out_specs = (
        pl.BlockSpec((1, ts, D_MODEL), lambda i, s: (i, s, 0)),
        pl.BlockSpec((1, KV_DIM, WINDOW), lambda i, s: (i, 0, 0)),
        pl.BlockSpec((1, KV_DIM, WINDOW), lambda i, s: (i, 0, 0)),
        pl.BlockSpec((1, CONV_WIDTH - 1, CONV_DIM), lambda i, s: (i, 0, 0)),
        pl.BlockSpec((1, MEM_DIM, N_MEM), lambda i, s: (i, 0, 0)),
        pl.BlockSpec((1, MEM_DIM, N_MEM), lambda i, s: (i, 0, 0)),
    )
    scratch = [
        pltpu.VMEM((WINDOW, KV_DIM), BF16),
        pltpu.VMEM((WINDOW, KV_DIM), BF16),
        pltpu.VMEM((8, CONV_DIM), F32),
        pltpu.VMEM((N_MEM, MEM_DIM), BF16),
        pltpu.VMEM((N_MEM, MEM_DIM), BF16),
    ]
    return pl.pallas_call(
        kern,
        grid=(b, seq // ts),
        in_specs=in_specs,
        out_specs=out_specs,
        out_shape=out_shape,
        scratch_shapes=scratch,
        compiler_params=pltpu.CompilerParams(dimension_semantics=("arbitrary", "arbitrary"),
                                             vmem_limit_bytes=VMEM_LIMIT),
        name=f"mixer_prompt_l{l}",
    )(w["sink"], x, mem, w["w_in"], w["conv_w"], w["p_attn"], w["p_conv"], w["p_mem"], w["w_o"],
      w["w_mem_k"], w["w_mem_v"], w["ln1_g"], w["ln1_b"])


def _mixer_sample_kernel(sink_ref, x_ref, ck_ref, cvv_ref, cc_ref, cmk_ref, cmv_ref, w_in_ref, wkvt_ref, convw_ref,
                         p_attn_ref, p_conv_ref, p_mem_ref, w_o_ref, g_ref, b_ref,
                         x1_ref, wk_ref, wv_ref, cv_ref, *, bb, t, alpha):
    nbuf = ck_ref.shape[2]
    x = x_ref[...]
    xb = x.astype(BF16)
    z = _dot(xb, w_in_ref[:, 0:_MIX_COLS])
    q = z[:, _Q0:_K0] * ATTN_SCALE
    u = z[:, _GC0:_QM0] * z[:, _H0:_GB0]
    g_b = z[:, _GB0:_GC0]
    qm = z[:, _QM0:_GA0] * ATTN_SCALE
    cw = convw_ref[...]
    kvt = _dot_nt(wkvt_ref[...], xb)
    kt_new = kvt[0:KV_DIM]
    vt_new = kvt[KV_DIM:2 * KV_DIM]

    nstk = N_HEADS * t
    r_t = lax.broadcasted_iota(jnp.int32, (nstk, 2 * WINDOW), 0) % t
    c_j = lax.broadcasted_iota(jnp.int32, (nstk, 2 * WINDOW), 1)
    dist = r_t + nbuf - c_j
    wmask = (dist >= 0) & (dist < WINDOW) & (c_j < nbuf + t)
    blk_id = lax.broadcasted_iota(jnp.int32, (nstk, 1), 0) // t
    sink_col = jnp.zeros((nstk, 1), F32)
    for hh in range(N_HEADS):
        sink_col = jnp.where(blk_id == hh, sink_ref[_stack_head(hh)], sink_col)
    mlane = lax.broadcasted_iota(jnp.int32, (t, MEM_DIM), 1) // HEAD_DIM
    klane = lax.broadcasted_iota(jnp.int32, (KV_DIM, LANES), 1)

    o_a_rows, o_c_rows, o_m_rows = [], [], []
    for bi in range(bb):
        sl = slice(bi * t, (bi + 1) * t)
        kc = ck_ref[bi]
        vc = cvv_ref[bi]
        k_new0 = jnp.where(klane < t, pltpu.roll(kt_new, (LANES - bi * t) % LANES, 1), 0.0)
        v_new0 = jnp.where(klane < t, pltpu.roll(vt_new, (LANES - bi * t) % LANES, 1), 0.0)
        k_all = jnp.concatenate([kc, k_new0], axis=1).astype(BF16)
        v_all = jnp.concatenate([vc, v_new0], axis=1).astype(BF16)
        qs = _stack_lane_halves(q[sl], GROUP).astype(BF16)
        sc = jnp.where(wmask, _dot(qs, k_all), NEG_INF)
        m = jnp.maximum(jnp.max(sc, axis=-1, keepdims=True), sink_col)
        e = jnp.exp(sc - m)
        den = jnp.sum(e, axis=-1, keepdims=True) + jnp.exp(sink_col - m)
        r = _dot_nt((e / den).astype(BF16), v_all)
        o_a_rows.append(_unstack_lane_halves(r, t, GROUP))
        tail = (nbuf - t - bi * t) % LANES
        wk_ref[bi] = jnp.where(klane < nbuf - t, pltpu.roll(kc, nbuf - t, 1), pltpu.roll(kt_new, tail, 1))
        wv_ref[bi] = jnp.where(klane < nbuf - t, pltpu.roll(vc, nbuf - t, 1), pltpu.roll(vt_new, tail, 1))
        u_ext = jnp.concatenate([cc_ref[bi], u[sl]], axis=0)
        conv = cw[0:1, :] * u_ext[0:t] + cw[1:2, :] * u_ext[1:t + 1] + cw[2:3, :] * u_ext[2:t + 2]
        o_c_rows.append(g_b[sl] * conv)
        cv_ref[bi] = u_ext[t:t + CONV_WIDTH - 1]
        qmb = qm[sl]
        qms = jnp.concatenate([jnp.where(mlane == hh, qmb, 0.0) for hh in range(MEM_HEADS)], axis=0).astype(BF16)
        scm = _dot(qms, cmk_ref[bi].astype(BF16))
        mm = jnp.max(scm, axis=-1, keepdims=True)
        em = jnp.exp(scm - mm)
        pm = (em / jnp.sum(em, axis=-1, keepdims=True)).astype(BF16)
        rm = _dot_nt(pm, cmv_ref[bi].astype(BF16))
        om = jnp.zeros((t, MEM_DIM), F32)
        for hh in range(MEM_HEADS):
            om = jnp.where(mlane == hh, rm[hh * t:(hh + 1) * t], om)
        o_m_rows.append(om)

    o_a = jnp.concatenate(o_a_rows, axis=0)
    o_c = jnp.concatenate(o_c_rows, axis=0)
    o_m = jnp.concatenate(o_m_rows, axis=0)
    zg = _dot(xb, w_in_ref[:, _GA0:IN_DIM])
    x1_ref[...] = _merge_project_norm(x, zg, o_a, o_c, o_m, p_attn_ref, p_conv_ref, p_mem_ref, w_o_ref,
                                      g_ref, b_ref, alpha)


def _mixer_sample(x2d, ckt, cvt, cc, cmkt, cmvt, w, l, *, bb, t, alpha):
    nb, nbuf = ckt.shape[1], ckt.shape[3]
    assert nbuf == LANES and bb * t == LANES and t % 8 == 0
    kern = functools.partial(_mixer_sample_kernel, bb=bb, t=t, alpha=alpha)
    smem = pl.BlockSpec(memory_space=pltpu.SMEM)

    def cache_spec(d1, d2):
        return pl.BlockSpec((None, bb, d1, d2), lambda i: (l, i, 0, 0))

    in_specs = [
        smem,
        pl.BlockSpec((bb * t, D_MODEL), lambda i: (i, 0)),
        cache_spec(KV_DIM, nbuf),
        cache_spec(KV_DIM, nbuf),
        cache_spec(CONV_WIDTH - 1, CONV_DIM),
        cache_spec(MEM_DIM, N_MEM),
        cache_spec(MEM_DIM, N_MEM),
        _const_spec((D_MODEL, IN_DIM)),
        _const_spec((2 * KV_DIM, D_MODEL)),
        _const_spec((CONV_WIDTH, CONV_DIM)),
        _const_spec((ATTN_DIM, D_MODEL)),
        _const_spec((CONV_DIM, D_MODEL)),
        _const_spec((MEM_DIM, D_MODEL)),
        _const_spec((D_MODEL, D_MODEL)),
        _const_spec((1, D_MODEL)),
        _const_spec((1, D_MODEL)),
    ]
    out_shape = (
        jax.ShapeDtypeStruct((nb * t, D_MODEL), F32),
        jax.ShapeDtypeStruct((nb, KV_DIM, nbuf), F32),
        jax.ShapeDtypeStruct((nb, KV_DIM, nbuf), F32),
        jax.ShapeDtypeStruct((nb, CONV_WIDTH - 1, CONV_DIM), F32),
    )
    out_specs = (
        pl.BlockSpec((bb * t, D_MODEL), lambda i: (i, 0)),
        pl.BlockSpec((bb, KV_DIM, nbuf), lambda i: (i, 0, 0)),
        pl.BlockSpec((bb, KV_DIM, nbuf), lambda i: (i, 0, 0)),
        pl.BlockSpec((bb, CONV_WIDTH - 1, CONV_DIM), lambda i: (i, 0, 0)),
    )
    return pl.pallas_call(
        kern,
        grid=(nb // bb,),
        in_specs=in_specs,
        out_specs=out_specs,
        out_shape=out_shape,
        compiler_params=pltpu.CompilerParams(dimension_semantics=("arbitrary",),
                                             vmem_limit_bytes=VMEM_LIMIT),
        name=f"mixer_sample_l{l}",
    )(w["sink"], x2d, ckt, cvt, cc, cmkt, cmvt, w["w_in"], w["w_kvt"], w["conv_w"], w["p_attn"], w["p_conv"],
      w["p_mem"], w["w_o"], w["ln1_g"], w["ln1_b"])


def _swiglu(xb, wg, wu, wd):
    g = _dot(xb, wg)
    u = _dot(xb, wu)
    return _dot((jax.nn.silu(g) * u).astype(BF16), wd)


def _ffn_kernel(x_ref, wg_ref, wu_ref, wd_ref, g_ref, b_ref, o_ref, *, alpha):
    x = x_ref[...]
    y = _swiglu(x.astype(BF16), wg_ref[...], wu_ref[...], wd_ref[...])
    o_ref[...] = _layer_norm(alpha * x + y, g_ref[...], b_ref[...])


def _ffn_dense(x, wg, wu, wd, g, b, *, tm, alpha, name):
    n = x.shape[0]
    return pl.pallas_call(
        functools.partial(_ffn_kernel, alpha=alpha),
        grid=(n // tm,),
        in_specs=[pl.BlockSpec((tm, D_MODEL), lambda i: (i, 0)),
                  _const_spec((D_MODEL, D_FF)), _const_spec((D_MODEL, D_FF)), _const_spec((D_FF, D_MODEL)),
                  _const_spec((1, D_MODEL)), _const_spec((1, D_MODEL))],
        out_specs=pl.BlockSpec((tm, D_MODEL), lambda i: (i, 0)),
        out_shape=jax.ShapeDtypeStruct((n, D_MODEL), F32),
        compiler_params=pltpu.CompilerParams(dimension_semantics=("arbitrary",), vmem_limit_bytes=VMEM_LIMIT),
        name=name,
    )(x, wg, wu, wd, g, b)


def _two_group_specs(tm, n_a_tiles, n_b_tiles):
    spec_a = pl.BlockSpec((tm, D_MODEL), lambda i: (jnp.minimum(i, n_a_tiles - 1), 0))
    spec_b = pl.BlockSpec((tm, D_MODEL), lambda i: (jnp.clip(i - n_a_tiles, 0, n_b_tiles - 1), 0))
    return spec_a, spec_b


def _route_kernel(xa_ref, xb_ref, wrt_ref, rb_ref, tri_ref, slab_ref, cnt_ref, carry, *, tm, n_a_tiles):
    i = pl.program_id(0)

    @pl.when(i == 0)
    def _():
        carry[...] = jnp.zeros_like(carry)

    x = jnp.where(i < n_a_tiles, xa_ref[...], xb_ref[...])
    lg = _dot_nt(wrt_ref[...], x.astype(BF16)) + rb_ref[...]
    e_iota = lax.broadcasted_iota(jnp.int32, (N_EXPERTS, tm), 0)
    m1 = jnp.max(lg, axis=0, keepdims=True)
    i1 = jnp.min(jnp.where(lg == m1, e_iota, N_EXPERTS), axis=0, keepdims=True)
    lg2 = jnp.where(e_iota == i1, NEG_INF, lg)
    m2 = jnp.max(lg2, axis=0, keepdims=True)
    i2 = jnp.min(jnp.where(lg2 == m2, e_iota, N_EXPERTS), axis=0, keepdims=True)
    e2 = jnp.exp(m2 - m1)
    den = 1.0 + e2
    w1 = 1.0 / den
    w2 = e2 / den
    sel1 = e_iota == i1
    sel2 = e_iota == i2
    oh = jnp.where(sel1 | sel2, 1.0, 0.0)
    pre = _dot(oh.astype(BF16), tri_ref[...]) + carry[:, 0:1]
    r1 = jnp.sum(jnp.where(sel1, pre, 0.0), axis=0, keepdims=True)
    r2 = jnp.sum(jnp.where(sel2, pre, 0.0), axis=0, keepdims=True)
    carry[...] = carry[...] + jnp.sum(oh, axis=1, keepdims=True)
    rows = [i1.astype(F32), i2.astype(F32), r1, r2, w1, w2]
    slab = jnp.zeros((N_EXPERTS, tm), F32)
    for ri, rv in enumerate(rows):
        slab = jnp.where(e_iota == ri, rv, slab)
    slab_ref[...] = slab
    cnt_ref[...] = carry[...]


def _route(xa, xb, wrt, rb, *, tm):
    na, nb_ = xa.shape[0] // tm, xb.shape[0] // tm
    tri = jnp.triu(jnp.ones((tm, tm), BF16), k=1)
    spec_a, spec_b = _two_group_specs(tm, na, nb_)
    return pl.pallas_call(
        functools.partial(_route_kernel, tm=tm, n_a_tiles=na),
        grid=(na + nb_,),
        in_specs=[spec_a, spec_b,
                  _const_spec((N_EXPERTS, D_MODEL)), _const_spec((N_EXPERTS, 1)), _const_spec((tm, tm))],
        out_specs=(pl.BlockSpec((N_EXPERTS, tm), lambda i: (0, i)),
                   pl.BlockSpec((N_EXPERTS, LANES), lambda i: (0, 0))),
        out_shape=(jax.ShapeDtypeStruct((N_EXPERTS, (na + nb_) * tm), F32),
                   jax.ShapeDtypeStruct((N_EXPERTS, LANES), F32)),
        scratch_shapes=[pltpu.VMEM((N_EXPERTS, LANES), F32)],
        compiler_params=pltpu.CompilerParams(dimension_semantics=("arbitrary",)),
        name="moe_route",
    )(xa, xb, wrt, rb, tri)


def _row_copy(src, src_row, dst, dst_row, sem):
    return pltpu.make_async_copy(src.at[pl.ds(src_row, 1)], dst.at[pl.ds(dst_row, 1)], sem)


def _dispatch_kernel(slot_ref, xa_ref, xb_ref, z_ref, xs_ref, sem, *, tm, n_a_tiles, n_b_tiles):
    i = pl.program_id(0)

    def issue_from(src_ref):
        def issue(r, c):
            _row_copy(src_ref, r, xs_ref, slot_ref[0, 0, r], sem).start(priority=0)
            _row_copy(src_ref, r, xs_ref, slot_ref[0, 0, tm + r], sem).start(priority=1)
            return c

        lax.fori_loop(0, tm, issue, 0, unroll=8)

    @pl.when(i < n_a_tiles)
    def _():
        issue_from(xa_ref)

    @pl.when((i >= n_a_tiles) & (i < n_a_tiles + n_b_tiles))
    def _():
        issue_from(xb_ref)

    @pl.when(i >= n_a_tiles + n_b_tiles)
    def _():
        issue_from(z_ref)

    for _ in range(2):
        pltpu.make_async_copy(z_ref, xs_ref.at[pl.ds(0, tm)], sem).wait()


def _dispatch(xa, xb, slots3, n_rows, *, tm):
    na, nb_ = xa.shape[0] // tm, xb.shape[0] // tm
    nz = slots3.shape[0] - na - nb_
    spec_a, spec_b = _two_group_specs(tm, na, nb_)
    zsrc = jnp.zeros((tm, D_MODEL), F32)
    return pl.pallas_call(
        functools.partial(_dispatch_kernel, tm=tm, n_a_tiles=na, n_b_tiles=nb_),
        grid=(na + nb_ + nz,),
        in_specs=[pl.BlockSpec((1, 1, 2 * tm), lambda i: (i, 0, 0), memory_space=pltpu.SMEM),
                  spec_a, spec_b, _const_spec((tm, D_MODEL))],
        out_specs=pl.BlockSpec(memory_space=pl.ANY),
        out_shape=jax.ShapeDtypeStruct((n_rows, D_MODEL), F32),
        scratch_shapes=[pltpu.SemaphoreType.DMA(())],
        compiler_params=pltpu.CompilerParams(dimension_semantics=("arbitrary",)),
        name="moe_dispatch",
    )(slots3, xa, xb, zsrc)


def _expert_kernel(te_ref, nt_ref, xs_ref, wg_ref, wu_ref, wd_ref, o_ref):
    del te_ref
    live = pl.program_id(0) < nt_ref[0]

    @pl.when(live)
    def _():
        o_ref[...] = _swiglu(xs_ref[...].astype(BF16), wg_ref[0], wu_ref[0], wd_ref[0])

    @pl.when(jnp.logical_not(live))
    def _():
        o_ref[...] = jnp.zeros_like(o_ref)


def _experts(xs, tile_expert, n_tiles, wg, wu, wd, *, tm):
    n_rows = xs.shape[0]

    def row_map(i, te, nt):
        return (jnp.minimum(i, nt[0] - 1), 0)

    def w_map(i, te, nt):
        return (te[i], 0, 0)

    return pl.pallas_call(
        _expert_kernel,
        grid_spec=pltpu.PrefetchScalarGridSpec(
            num_scalar_prefetch=2,
            grid=(n_rows // tm,),
            in_specs=[pl.BlockSpec((tm, D_MODEL), row_map),
                      pl.BlockSpec((1, D_MODEL, D_FF), w_map),
                      pl.BlockSpec((1, D_MODEL, D_FF), w_map),
                      pl.BlockSpec((1, D_FF, D_MODEL), w_map)],
            out_specs=pl.BlockSpec((tm, D_MODEL), lambda i, te, nt: (i, 0)),
        ),
        out_shape=jax.ShapeDtypeStruct((n_rows, D_MODEL), F32),
        compiler_params=pltpu.CompilerParams(dimension_semantics=("arbitrary",), vmem_limit_bytes=VMEM_LIMIT),
        name="moe_experts",
    )(tile_expert, n_tiles, xs, wg, wu, wd)


def _combine_kernel(slot_ref, x_ref, slab_ref, ys_ref, g_ref, b_ref, o_ref, buf0, buf1, sem, *, tm, alpha):
    def issue(r, c):
        _row_copy(ys_ref, slot_ref[0, 0, r], buf0, r, sem).start(priority=0)
        _row_copy(ys_ref, slot_ref[0, 0, tm + r], buf1, r, sem).start(priority=1)
        return c

    lax.fori_loop(0, tm, issue, 0, unroll=8)

    pltpu.make_async_copy(ys_ref.at[pl.ds(0, tm)], buf0, sem).wait()
    pltpu.make_async_copy(ys_ref.at[pl.ds(0, tm)], buf1, sem).wait()
    slab_t = jnp.concatenate([slab_ref[...], jnp.zeros((LANES - N_EXPERTS, tm), F32)], axis=0).T
    y = slab_t[:, 4:5] * buf0[...] + slab_t[:, 5:6] * buf1[...]
    o_ref[...] = _layer_norm(alpha * x_ref[...] + y, g_ref[...], b_ref[...])


def _combine(x, slots3, slab, ys, g, b, *, tm, tile0, alpha, name):
    n = x.shape[0]
    return pl.pallas_call(
        functools.partial(_combine_kernel, tm=tm, alpha=alpha),
        grid=(n // tm,),
        in_specs=[pl.BlockSpec((1, 1, 2 * tm), lambda i: (i + tile0, 0, 0), memory_space=pltpu.SMEM),
                  pl.BlockSpec((tm, D_MODEL), lambda i: (i, 0)),
                  pl.BlockSpec((N_EXPERTS, tm), lambda i: (0, i + tile0)),
                  pl.BlockSpec(memory_space=pl.ANY),
                  _const_spec((1, D_MODEL)), _const_spec((1, D_MODEL))],
        out_specs=pl.BlockSpec((tm, D_MODEL), lambda i: (i, 0)),
        out_shape=jax.ShapeDtypeStruct((n, D_MODEL), F32),
        scratch_shapes=[pltpu.VMEM((tm, D_MODEL), F32), pltpu.VMEM((tm, D_MODEL), F32),
                        pltpu.SemaphoreType.DMA(())],
        compiler_params=pltpu.CompilerParams(dimension_semantics=("arbitrary",)),
        name=name,
    )(slots3, x, slab, ys, g, b)


def _free_slots(counts, offs, padded, n_rows, n_free):
    starts = jnp.concatenate([offs + counts, (offs[-1:] + padded[-1:])])
    gaps = jnp.concatenate([padded - counts, n_rows - (offs[-1:] + padded[-1:])])
    cum = jnp.cumsum(gaps)
    prev = cum - gaps
    idx = jnp.arange(n_free, dtype=jnp.int32)
    seg = jnp.sum(idx[:, None] >= cum[None, :], axis=1)
    pick = seg[:, None] == jnp.arange(N_EXPERTS + 1, dtype=jnp.int32)[None, :]
    return jnp.sum(jnp.where(pick, (starts - prev)[None, :], 0), axis=1) + idx


def _moe(xa, xb, router_w, router_b, wg, wu, wd, g, b, *, alpha, tm_rows, tm_expert):
    na, nb_ = xa.shape[0], xb.shape[0]
    n = na + nb_
    slab, cnt = _route(xa, xb, router_w.T.astype(BF16), router_b.reshape(N_EXPERTS, 1), tm=tm_rows)
    counts = cnt[:, 0].astype(jnp.int32)
    tiles_per = (counts + tm_expert - 1) // tm_expert
    padded = tiles_per * tm_expert
    ends = jnp.cumsum(tiles_per)
    offs = (ends - tiles_per) * tm_expert
    i1 = slab[0].astype(jnp.int32)
    i2 = slab[1].astype(jnp.int32)
    eye = jnp.arange(N_EXPERTS, dtype=jnp.int32)[:, None]
    off1 = jnp.sum(jnp.where(eye == i1[None, :], offs[:, None], 0), axis=0)
    off2 = jnp.sum(jnp.where(eye == i2[None, :], offs[:, None], 0), axis=0)
    slots = jnp.concatenate([(off1 + slab[2].astype(jnp.int32)).reshape(-1, 1, tm_rows),
                             (off2 + slab[3].astype(jnp.int32)).reshape(-1, 1, tm_rows)], axis=2)
    n_free = N_EXPERTS * tm_expert
    n_rows = 2 * n + n_free
    assert n_free % (2 * tm_rows) == 0
    free = _free_slots(counts, offs, padded, n_rows, n_free)
    max_tiles = n_rows // tm_expert
    tile_expert = jnp.minimum(
        jnp.sum(jnp.arange(max_tiles, dtype=jnp.int32)[:, None] >= ends[None, :], axis=1), N_EXPERTS - 1
    ).astype(jnp.int32)
    n_tiles = ends[-1:].astype(jnp.int32)
    slots3 = jnp.concatenate([slots, free.reshape(-1, 1, 2 * tm_rows)], axis=0)

    xs = _dispatch(xa, xb, slots3, n_rows, tm=tm_rows)
    ys = _experts(xs, tile_expert, n_tiles, wg, wu, wd, tm=tm_expert)
    ya = _combine(xa, slots3, slab, ys, g, b, tm=tm_rows, tile0=0, alpha=alpha, name="moe_combine_a")
    yb = _combine(xb, slots3, slab, ys, g, b, tm=tm_rows, tile0=na // tm_rows, alpha=alpha, name="moe_combine_b")
    return ya, yb


def _layer_weights(l, w_in, conv_w, sink, p_attn, p_conv, p_mem, w_o, w_mem_k, w_mem_v, ln1_g, ln1_b):
    w_in_l = w_in[l].astype(BF16)
    w_q = w_in_l[:, :ATTN_DIM].reshape(D_MODEL, N_KV_HEADS, GROUP, HEAD_DIM)
    w_q = jnp.transpose(w_q, (0, 2, 1, 3)).reshape(D_MODEL, ATTN_DIM)
    p_a = p_attn[l].astype(BF16).reshape(N_KV_HEADS, GROUP, HEAD_DIM, D_MODEL)
    p_a = jnp.transpose(p_a, (1, 0, 2, 3)).reshape(ATTN_DIM, D_MODEL)
    return dict(
        sink=sink[l],
        w_in=jnp.concatenate([w_q, w_in_l[:, ATTN_DIM:]], axis=1),
        w_kvt=w_in_l[:, _K0:_H0].T,
        conv_w=conv_w[l],
        p_attn=p_a,
        p_conv=p_conv[l].astype(BF16),
        p_mem=p_mem[l].astype(BF16),
        w_o=w_o[l].astype(BF16),
        w_mem_k=w_mem_k[l].astype(BF16),
        w_mem_v=w_mem_v[l].astype(BF16),
        ln1_g=ln1_g[l].reshape(1, D_MODEL),
        ln1_b=ln1_b[l].reshape(1, D_MODEL),
    )


def _feature_major(c):
    d, nb, tok, h, hd = c.shape
    return jnp.transpose(c, (0, 1, 3, 4, 2)).reshape(d, nb, h * hd, tok)


def _token_major(c, heads):
    d, nb, f, tok = c.shape
    return jnp.transpose(c.reshape(d, nb, heads, f // heads, tok), (0, 1, 4, 2, 3))


def kernel(x_prompt, x_sample, mem_prompt, cache_win_k, cache_win_v, cache_conv, cache_mem_k, cache_mem_v, w_in, conv_w, sink, p_attn, p_conv, p_mem, w_o, w_mem_k, w_mem_v, ln1_g, ln1_b, ln2_g, ln2_b, ffn_w_gate, ffn_w_up, ffn_w_down, router_w, router_b, exp_w_gate, exp_w_up, exp_w_down):
    depth = w_in.shape[0]
    alpha = (2 * depth) ** 0.25
    b, seq, _ = x_prompt.shape
    nb, t, _ = x_sample.shape
    n_p = b * seq
    n_s = nb * t

    xp = x_prompt
    xs = x_sample.reshape(n_s, D_MODEL)
    ckt = _feature_major(cache_win_k)
    cvt = _feature_major(cache_win_v)
    cmkt = _feature_major(cache_mem_k)
    cmvt = _feature_major(cache_mem_v)

    outs = {k_: [] for k_ in ("wk_p", "wv_p", "cv_p", "mk_p", "mv_p", "wk_s", "wv_s", "cv_s")}
    for l in range(depth):
        w = _layer_weights(l, w_in, conv_w, sink, p_attn, p_conv, p_mem, w_o, w_mem_k, w_mem_v, ln1_g, ln1_b)
        x1p, wk, wv, cvp, mk, mv = _mixer_prompt(xp, mem_prompt, w, l, ts=1024, alpha=alpha)
        x1s, wks, wvs, cvs = _mixer_sample(xs, ckt, cvt, cache_conv, cmkt, cmvt, w, l, bb=LANES // t, t=t, alpha=alpha)
        outs["wk_p"].append(wk); outs["wv_p"].append(wv); outs["cv_p"].append(cvp)
        outs["mk_p"].append(mk); outs["mv_p"].append(mv)
        outs["wk_s"].append(wks); outs["wv_s"].append(wvs); outs["cv_s"].append(cvs)
        x1p = x1p.reshape(n_p, D_MODEL)
        g2 = ln2_g[l].reshape(1, D_MODEL)
        b2 = ln2_b[l].reshape(1, D_MODEL)
        i = l // 2
        if l % 2 == 0:
            wg, wu, wd = ffn_w_gate[i].astype(BF16), ffn_w_up[i].astype(BF16), ffn_w_down[i].astype(BF16)
            x2p = _ffn_dense(x1p, wg, wu, wd, g2, b2, tm=512, alpha=alpha, name=f"ffn_dense_p_l{l}")
            xs = _ffn_dense(x1s, wg, wu, wd, g2, b2, tm=512, alpha=alpha, name=f"ffn_dense_s_l{l}")
        else:
            x2p, xs = _moe(x1p, x1s, router_w[i], router_b[i], exp_w_gate[i].astype(BF16), exp_w_up[i].astype(BF16),
                           exp_w_down[i].astype(BF16), g2, b2, alpha=alpha, tm_rows=512, tm_expert=256)
        xp = x2p.reshape(b, seq, D_MODEL)

    return (xp, xs.reshape(nb, t, D_MODEL),
            _token_major(jnp.stack(outs["wk_p"]), N_KV_HEADS),
            _token_major(jnp.stack(outs["wv_p"]), N_KV_HEADS),
            jnp.stack(outs["cv_p"]),
            _token_major(jnp.stack(outs["mk_p"]), MEM_HEADS),
            _token_major(jnp.stack(outs["mv_p"]), MEM_HEADS),
            _token_major(jnp.stack(outs["wk_s"]), N_KV_HEADS),
            _token_major(jnp.stack(outs["wv_s"]), N_KV_HEADS),
            jnp.stack(outs["cv_s"]))
```

```python
import functools

import jax
import jax.numpy as jnp
from jax import lax
from jax.experimental import pallas as pl
from jax.experimental.pallas import tpu as pltpu

F32 = jnp.float32
BF16 = jnp.bfloat16

D_MODEL = 1024
HEAD_DIM = 64
N_HEADS = 8
N_KV_HEADS = 2
GROUP = N_HEADS // N_KV_HEADS
ATTN_DIM = N_HEADS * HEAD_DIM
KV_DIM = N_KV_HEADS * HEAD_DIM
WINDOW = 128
ATTN_SCALE = HEAD_DIM ** -0.5
CONV_DIM = 256
CONV_WIDTH = 3
N_MEM = 256
MEM_HEADS = 4
MEM_DIM = MEM_HEADS * HEAD_DIM
D_FF = 2816
N_EXPERTS = 8
LN_EPS = 1e-5

_Q0, _K0, _V0, _H0, _GB0, _GC0, _QM0, _GA0 = 0, 512, 640, 768, 1024, 1280, 1536, 1792
_MIX_COLS = _GA0
IN_DIM = _GA0 + 3 * D_MODEL

LANES = 128
VMEM_LIMIT = 56 * 1024 * 1024

NEG_INF = float("-inf")


def _dup_kv_lanes(kv):
    lo = lax.broadcasted_iota(jnp.int32, kv.shape, 1) < HEAD_DIM
    swapped = pltpu.roll(kv, HEAD_DIM, 1)
    return [jnp.where(lo, kv, swapped).astype(BF16), jnp.where(lo, swapped, kv).astype(BF16)]


def _layer_norm(v, g, b):
    mu = jnp.mean(v, axis=-1, keepdims=True)
    c = v - mu
    var = jnp.mean(c * c, axis=-1, keepdims=True)
    return c * lax.rsqrt(var + LN_EPS) * g + b


def _dot(a, b):
    return jnp.dot(a, b, preferred_element_type=F32)


def _dot_nt(a, b):
    return lax.dot_general(a, b, (((1,), (1,)), ((), ())), preferred_element_type=F32)


def _stack_lane_halves(q, n_blocks):
    m = q.shape[0]
    lo = lax.broadcasted_iota(jnp.int32, (m, LANES), 1) < HEAD_DIM
    parts = []
    for j in range(n_blocks):
        blk = q[:, j * LANES:(j + 1) * LANES]
        parts.append(jnp.where(lo, blk, 0.0))
        parts.append(jnp.where(lo, 0.0, blk))
    return jnp.concatenate(parts, axis=0)


def _unstack_lane_halves(r, m, n_blocks):
    lo = lax.broadcasted_iota(jnp.int32, (m, LANES), 1) < HEAD_DIM
    cols = [jnp.where(lo, r[(2 * j) * m:(2 * j + 1) * m], r[(2 * j + 1) * m:(2 * j + 2) * m])
            for j in range(n_blocks)]
    return jnp.concatenate(cols, axis=1)


def _merge_project_norm(x, zg, o_a, o_c, o_m, p_attn_ref, p_conv_ref, p_mem_ref, w_o_ref, g_ref, b_ref, alpha):
    merged = (jax.nn.sigmoid(zg[:, 0:D_MODEL]) * _dot(o_a.astype(BF16), p_attn_ref[...])
              + jax.nn.sigmoid(zg[:, D_MODEL:2 * D_MODEL]) * _dot(o_c.astype(BF16), p_conv_ref[...])
              + jax.nn.sigmoid(zg[:, 2 * D_MODEL:3 * D_MODEL]) * _dot(o_m.astype(BF16), p_mem_ref[...]))
    mix = _dot(merged.astype(BF16), w_o_ref[...])
    return _layer_norm(alpha * x + mix, g_ref[...], b_ref[...])


def _const_spec(shape):
    n = len(shape)
    return pl.BlockSpec(shape, lambda *_: (0,) * n, pipeline_mode=pl.Buffered(1))


def _mixer_prompt_kernel(sink_ref, x_ref, mem_ref, w_in_ref, convw_ref, p_attn_ref, p_conv_ref, p_mem_ref,
                         w_o_ref, wmk_ref, wmv_ref, g_ref, b_ref,
                         x1_ref, wk_ref, wv_ref, cv_ref, mk_ref, mv_ref,
                         kprev, vprev, uprev, mk_s, mv_s, *, ts, alpha):
    s = pl.program_id(1)
    ns = pl.num_programs(1)
    x = x_ref[0]
    xb = x.astype(BF16)

    @pl.when(s == 0)
    def _():
        memb = mem_ref[0].astype(BF16)
        mk = _dot(memb, wmk_ref[...])
        mv = _dot(memb, wmv_ref[...])
        mk_ref[0] = mk.T
        mv_ref[0] = mv.T
        mk_s[...] = mk.astype(BF16)
        mv_s[...] = mv.astype(BF16)
        kprev[...] = jnp.zeros_like(kprev)
        vprev[...] = jnp.zeros_like(vprev)
        uprev[...] = jnp.zeros_like(uprev)

    z = _dot(xb, w_in_ref[:, 0:_MIX_COLS])
    q = z[:, _Q0:_K0] * ATTN_SCALE
    k = z[:, _K0:_V0]
    v = z[:, _V0:_H0]
    hc = z[:, _H0:_GB0]
    g_b = z[:, _GB0:_GC0]
    g_c = z[:, _GC0:_QM0]
    qm = z[:, _QM0:_GA0] * ATTN_SCALE

    k_ext = jnp.concatenate([kprev[...], k], axis=0)
    v_ext = jnp.concatenate([vprev[...], v], axis=0)
    kd = _dup_kv_lanes(k_ext)
    vd = _dup_kv_lanes(v_ext)
    half = GROUP * WINDOW
    row_i = lax.broadcasted_iota(jnp.int32, (WINDOW, 2 * WINDOW), 0)
    col_j = lax.broadcasted_iota(jnp.int32, (WINDOW, 2 * WINDOW), 1)
    band = (col_j > row_i) & (col_j <= row_i + WINDOW)
    band_first = band & ((col_j >= WINDOW) | (s > 0))
    o_blocks = []
    for blk in range(ts // WINDOW):
        qs = _stack_lane_halves(q[blk * WINDOW:(blk + 1) * WINDOW], GROUP).astype(BF16)
        band_rows = slice(blk * WINDOW, blk * WINDOW + 2 * WINDOW)
        sc = jnp.concatenate([_dot_nt(qs[gi * half:(gi + 1) * half], kd[gi][band_rows])
                              for gi in range(N_KV_HEADS)], axis=0)
        mask = band_first if blk == 0 else band
        ps = []
        for hh in range(N_HEADS):
            sink_h = sink_ref[hh]
            sch = jnp.where(mask, sc[hh * WINDOW:(hh + 1) * WINDOW], NEG_INF)
            m = jnp.maximum(jnp.max(sch, axis=-1, keepdims=True), sink_h)
            e = jnp.exp(sch - m)
            den = jnp.sum(e, axis=-1, keepdims=True) + jnp.exp(sink_h - m)
            ps.append((e / den).astype(BF16))
        r = jnp.concatenate([_dot(jnp.concatenate(ps[gi * GROUP:(gi + 1) * GROUP], axis=0), vd[gi][band_rows])
                             for gi in range(N_KV_HEADS)], axis=0)
        o_blocks.append(_unstack_lane_halves(r, WINDOW, GROUP))
    o_a = jnp.concatenate(o_blocks, axis=0)
    kprev[...] = k_ext[ts:ts + WINDOW]
    vprev[...] = v_ext[ts:ts + WINDOW]

    u = g_c * hc
    rows = lax.broadcasted_iota(jnp.int32, (ts, CONV_DIM), 0)
    up1 = uprev[7:8, :]
    up2 = uprev[6:7, :]
    u_m1 = jnp.where(rows == 0, up1, pltpu.roll(u, 1, 0))
    u_m2 = jnp.where(rows == 0, up2, jnp.where(rows == 1, up1, pltpu.roll(u, 2, 0)))
    cw = convw_ref[...]
    conv = cw[0:1, :] * u_m2 + cw[1:2, :] * u_m1 + cw[2:3, :] * u
    o_c = g_b * conv
    uprev[...] = u[ts - 8:ts]

    qms = _stack_lane_halves(qm, MEM_HEADS // 2).astype(BF16)
    o_cols = []
    lo = lax.broadcasted_iota(jnp.int32, (ts, LANES), 1) < HEAD_DIM
    for jb in range(MEM_HEADS // 2):
        scm = _dot_nt(qms[2 * jb * ts:(2 * jb + 2) * ts], mk_s[:, jb * LANES:(jb + 1) * LANES])
        mm = jnp.max(scm, axis=-1, keepdims=True)
        em = jnp.exp(scm - mm)
        pm = (em / jnp.sum(em, axis=-1, keepdims=True)).astype(BF16)
        rm = _dot(pm, mv_s[:, jb * LANES:(jb + 1) * LANES])
        o_cols.append(jnp.where(lo, rm[0:ts], rm[ts:2 * ts]))
    o_m = jnp.concatenate(o_cols, axis=1)

    zg = _dot(xb, w_in_ref[:, _GA0:IN_DIM])
    x1_ref[0] = _merge_project_norm(x, zg, o_a, o_c, o_m, p_attn_ref, p_conv_ref, p_mem_ref, w_o_ref,
                                    g_ref, b_ref, alpha)

    @pl.when(s == ns - 1)
    def _():
        wk_ref[0] = k[ts - WINDOW:ts].T
        wv_ref[0] = v[ts - WINDOW:ts].T
        cv_ref[0] = u[ts - (CONV_WIDTH - 1):ts]


def _mixer_prompt(x, mem, w, l, *, ts, alpha):
    b, seq, _ = x.shape
    kern = functools.partial(_mixer_prompt_kernel, ts=ts, alpha=alpha)
    smem = pl.BlockSpec(memory_space=pltpu.SMEM)
    in_specs = [
        smem,
        pl.BlockSpec((1, ts, D_MODEL), lambda i, s: (i, s, 0)),
        pl.BlockSpec((1, N_MEM, D_MODEL), lambda i, s: (i, 0, 0)),
        _const_spec((D_MODEL, IN_DIM)),
        _const_spec((CONV_WIDTH, CONV_DIM)),
        _const_spec((ATTN_DIM, D_MODEL)),
        _const_spec((CONV_DIM, D_MODEL)),
        _const_spec((MEM_DIM, D_MODEL)),
        _const_spec((D_MODEL, D_MODEL)),
        _const_spec((D_MODEL, MEM_DIM)),
        _const_spec((D_MODEL, MEM_DIM)),
        _const_spec((1, D_MODEL)),
        _const_spec((1, D_MODEL)),
    ]
    out_shape = (
        jax.ShapeDtypeStruct((b, seq, D_MODEL), F32),
        jax.ShapeDtypeStruct((b, KV_DIM, WINDOW), F32),
        jax.ShapeDtypeStruct((b, KV_DIM, WINDOW), F32),
        jax.ShapeDtypeStruct((b, CONV_WIDTH - 1, CONV_DIM), F32),
        jax.ShapeDtypeStruct((b, MEM_DIM, N_MEM), F32),
        jax.ShapeDtypeStruct((b, MEM_DIM, N_MEM), F32),
    )
    out_specs = (
        pl.BlockSpec((1, ts, D_MODEL), lambda i, s: (i, s, 0)),
        pl.BlockSpec((1, KV_DIM, WINDOW), lambda i, s: (i, 0, 0)),
        pl.BlockSpec((1, KV_DIM, WINDOW), lambda i, s: (i, 0, 0)),
        pl.BlockSpec((1, CONV_WIDTH - 1, CONV_DIM), lambda i, s: (i, 0, 0)),
        pl.BlockSpec((1, MEM_DIM, N_MEM), lambda i, s: (i, 0, 0)),
        pl.BlockSpec((1, MEM_DIM, N_MEM), lambda i, s: (i, 0, 0)),
    )
    scratch = [
        pltpu.VMEM((WINDOW, KV_DIM), F32),
        pltpu.VMEM((WINDOW, KV_DIM), F32),
        pltpu.VMEM((8, CONV_DIM), F32),
        pltpu.VMEM((N_MEM, MEM_DIM), BF16),
        pltpu.VMEM((N_MEM, MEM_DIM), BF16),
    ]
    return pl.pallas_call(
        kern,
        grid=(b, seq // ts),
        in_specs=in_specs,
        out_specs=out_specs,
        out_shape=out_shape,
        scratch_shapes=scratch,
        compiler_params=pltpu.CompilerParams(dimension_semantics=("arbitrary", "arbitrary"),
                                             vmem_limit_bytes=VMEM_LIMIT),
        name=f"mixer_prompt_l{l}",
    )(w["sink"], x, mem, w["w_in"], w["conv_w"], w["p_attn"], w["p_conv"], w["p_mem"], w["w_o"],
      w["w_mem_k"], w["w_mem_v"], w["ln1_g"], w["ln1_b"])


def _mixer_sample_kernel(sink_ref, x_ref, ck_ref, cvv_ref, cc_ref, cmk_ref, cmv_ref, w_in_ref, convw_ref,
                         p_attn_ref, p_conv_ref, p_mem_ref, w_o_ref, g_ref, b_ref,
                         x1_ref, wk_ref, wv_ref, cv_ref, *, bb, t, alpha):
    nbuf = ck_ref.shape[2]
    x = x_ref[...]
    xb = x.astype(BF16)
    z = _dot(xb, w_in_ref[:, 0:_MIX_COLS])
    q = z[:, _Q0:_K0] * ATTN_SCALE
    u = z[:, _GC0:_QM0] * z[:, _H0:_GB0]
    g_b = z[:, _GB0:_GC0]
    qm = z[:, _QM0:_GA0] * ATTN_SCALE
    cw = convw_ref[...]
    kt_new = z[:, _K0:_V0].T
    vt_new = z[:, _V0:_H0].T

    def dup_rows(kv_t):
        kvb = kv_t.astype(BF16)
        return [jnp.concatenate([kvb[gi * HEAD_DIM:(gi + 1) * HEAD_DIM]] * 2, axis=0) for gi in range(N_KV_HEADS)]

    nstk = N_HEADS * t
    r_t = lax.broadcasted_iota(jnp.int32, (nstk, 2 * WINDOW), 0) % t
    c_j = lax.broadcasted_iota(jnp.int32, (nstk, 2 * WINDOW), 1)
    dist = r_t + nbuf - c_j
    wmask = (dist >= 0) & (dist < WINDOW) & (c_j < nbuf + t)
    blk_id = lax.broadcasted_iota(jnp.int32, (nstk, 1), 0) // t
    sink_col = jnp.zeros((nstk, 1), F32)
    for hh in range(N_HEADS):
        sink_col = jnp.where(blk_id == hh, sink_ref[hh], sink_col)
    mlane = lax.broadcasted_iota(jnp.int32, (t, MEM_DIM), 1) // HEAD_DIM
    klane = lax.broadcasted_iota(jnp.int32, (KV_DIM, LANES), 1)

    o_a_rows, o_c_rows, o_m_rows = [], [], []
    for bi in range(bb):
        sl = slice(bi * t, (bi + 1) * t)
        kc = ck_ref[bi]
        vc = cvv_ref[bi]
        k_new0 = jnp.where(klane < t, pltpu.roll(kt_new, (LANES - bi * t) % LANES, 1), 0.0)
        v_new0 = jnp.where(klane < t, pltpu.roll(vt_new, (LANES - bi * t) % LANES, 1), 0.0)
        kd = dup_rows(jnp.concatenate([kc, k_new0], axis=1))
        vd = dup_rows(jnp.concatenate([vc, v_new0], axis=1))
        qs = _stack_lane_halves(q[sl], GROUP).astype(BF16)
        hrows = GROUP * t
        sc = jnp.concatenate([_dot(qs[gi * hrows:(gi + 1) * hrows], kd[gi]) for gi in range(N_KV_HEADS)], axis=0)
        sc = jnp.where(wmask, sc, NEG_INF)
        m = jnp.maximum(jnp.max(sc, axis=-1, keepdims=True), sink_col)
        e = jnp.exp(sc - m)
        den = jnp.sum(e, axis=-1, keepdims=True) + jnp.exp(sink_col - m)
        p = (e / den).astype(BF16)
        r = jnp.concatenate([_dot_nt(p[gi * hrows:(gi + 1) * hrows], vd[gi]) for gi in range(N_KV_HEADS)], axis=0)
        o_a_rows.append(_unstack_lane_halves(r, t, GROUP))
        tail = (nbuf - t - bi * t) % LANES
        wk_ref[bi] = jnp.where(klane < nbuf - t, pltpu.roll(kc, nbuf - t, 1), pltpu.roll(kt_new, tail, 1))
        wv_ref[bi] = jnp.where(klane < nbuf - t, pltpu.roll(vc, nbuf - t, 1), pltpu.roll(vt_new, tail, 1))
        u_ext = jnp.concatenate([cc_ref[bi], u[sl]], axis=0)
        conv = cw[0:1, :] * u_ext[0:t] + cw[1:2, :] * u_ext[1:t + 1] + cw[2:3, :] * u_ext[2:t + 2]
        o_c_rows.append(g_b[sl] * conv)
        cv_ref[bi] = u_ext[t:t + CONV_WIDTH - 1]
        qmb = qm[sl]
        qms = jnp.concatenate([jnp.where(mlane == hh, qmb, 0.0) for hh in range(MEM_HEADS)], axis=0).astype(BF16)
        scm = _dot(qms, cmk_ref[bi].astype(BF16))
        mm = jnp.max(scm, axis=-1, keepdims=True)
        em = jnp.exp(scm - mm)
        pm = (em / jnp.sum(em, axis=-1, keepdims=True)).astype(BF16)
        rm = _dot_nt(pm, cmv_ref[bi].astype(BF16))
        om = jnp.zeros((t, MEM_DIM), F32)
        for hh in range(MEM_HEADS):
            om = jnp.where(mlane == hh, rm[hh * t:(hh + 1) * t], om)
        o_m_rows.append(om)

    o_a = jnp.concatenate(o_a_rows, axis=0)
    o_c = jnp.concatenate(o_c_rows, axis=0)
    o_m = jnp.concatenate(o_m_rows, axis=0)
    zg = _dot(xb, w_in_ref[:, _GA0:IN_DIM])
    x1_ref[...] = _merge_project_norm(x, zg, o_a, o_c, o_m, p_attn_ref, p_conv_ref, p_mem_ref, w_o_ref,
                                      g_ref, b_ref, alpha)


def _mixer_sample(x2d, ckt, cvt, cc, cmkt, cmvt, w, l, *, bb, t, alpha):
    nb, nbuf = ckt.shape[1], ckt.shape[3]
    assert nbuf == LANES and bb * t == LANES and t % 8 == 0
    kern = functools.partial(_mixer_sample_kernel, bb=bb, t=t, alpha=alpha)
    smem = pl.BlockSpec(memory_space=pltpu.SMEM)

    def cache_spec(d1, d2):
        return pl.BlockSpec((None, bb, d1, d2), lambda i: (l, i, 0, 0))

    in_specs = [
        smem,
        pl.BlockSpec((bb * t, D_MODEL), lambda i: (i, 0)),
        cache_spec(KV_DIM, nbuf),
        cache_spec(KV_DIM, nbuf),
        cache_spec(CONV_WIDTH - 1, CONV_DIM),
        cache_spec(MEM_DIM, N_MEM),
        cache_spec(MEM_DIM, N_MEM),
        _const_spec((D_MODEL, IN_DIM)),
        _const_spec((CONV_WIDTH, CONV_DIM)),
        _const_spec((ATTN_DIM, D_MODEL)),
        _const_spec((CONV_DIM, D_MODEL)),
        _const_spec((MEM_DIM, D_MODEL)),
        _const_spec((D_MODEL, D_MODEL)),
        _const_spec((1, D_MODEL)),
        _const_spec((1, D_MODEL)),
    ]
    out_shape = (
        jax.ShapeDtypeStruct((nb * t, D_MODEL), F32),
        jax.ShapeDtypeStruct((nb, KV_DIM, nbuf), F32),
        jax.ShapeDtypeStruct((nb, KV_DIM, nbuf), F32),
        jax.ShapeDtypeStruct((nb, CONV_WIDTH - 1, CONV_DIM), F32),
    )
    out_specs = (
        pl.BlockSpec((bb * t, D_MODEL), lambda i: (i, 0)),
        pl.BlockSpec((bb, KV_DIM, nbuf), lambda i: (i, 0, 0)),
        pl.BlockSpec((bb, KV_DIM, nbuf), lambda i: (i, 0, 0)),
        pl.BlockSpec((bb, CONV_WIDTH - 1, CONV_DIM), lambda i: (i, 0, 0)),
    )
    return pl.pallas_call(
        kern,
        grid=(nb // bb,),
        in_specs=in_specs,
        out_specs=out_specs,
        out_shape=out_shape,
        compiler_params=pltpu.CompilerParams(dimension_semantics=("arbitrary",),
                                             vmem_limit_bytes=VMEM_LIMIT),
        name=f"mixer_sample_l{l}",
    )(w["sink"], x2d, ckt, cvt, cc, cmkt, cmvt, w["w_in"], w["conv_w"], w["p_attn"], w["p_conv"],
      w["p_mem"], w["w_o"], w["ln1_g"], w["ln1_b"])


def _swiglu(xb, wg, wu, wd):
    g = _dot(xb, wg)
    u = _dot(xb, wu)
    return _dot((jax.nn.silu(g) * u).astype(BF16), wd)


def _ffn_kernel(x_ref, wg_ref, wu_ref, wd_ref, g_ref, b_ref, o_ref, *, alpha):
    x = x_ref[...]
    y = _swiglu(x.astype(BF16), wg_ref[...], wu_ref[...], wd_ref[...])
    o_ref[...] = _layer_norm(alpha * x + y, g_ref[...], b_ref[...])


def _ffn_dense(x, wg, wu, wd, g, b, *, tm, alpha, name):
    n = x.shape[0]
    return pl.pallas_call(
        functools.partial(_ffn_kernel, alpha=alpha),
        grid=(n // tm,),
        in_specs=[pl.BlockSpec((tm, D_MODEL), lambda i: (i, 0)),
                  _const_spec((D_MODEL, D_FF)), _const_spec((D_MODEL, D_FF)), _const_spec((D_FF, D_MODEL)),
                  _const_spec((1, D_MODEL)), _const_spec((1, D_MODEL))],
        out_specs=pl.BlockSpec((tm, D_MODEL), lambda i: (i, 0)),
        out_shape=jax.ShapeDtypeStruct((n, D_MODEL), F32),
        compiler_params=pltpu.CompilerParams(dimension_semantics=("arbitrary",), vmem_limit_bytes=VMEM_LIMIT),
        name=name,
    )(x, wg, wu, wd, g, b)


def _two_group_specs(tm, n_a_tiles, n_b_tiles):
    spec_a = pl.BlockSpec((tm, D_MODEL), lambda i: (jnp.minimum(i, n_a_tiles - 1), 0))
    spec_b = pl.BlockSpec((tm, D_MODEL), lambda i: (jnp.clip(i - n_a_tiles, 0, n_b_tiles - 1), 0))
    return spec_a, spec_b


def _route_kernel(xa_ref, xb_ref, wrt_ref, rb_ref, tri_ref, slab_ref, cnt_ref, carry, *, tm, n_a_tiles):
    i = pl.program_id(0)

    @pl.when(i == 0)
    def _():
        carry[...] = jnp.zeros_like(carry)

    x = jnp.where(i < n_a_tiles, xa_ref[...], xb_ref[...])
    lg = _dot_nt(wrt_ref[...], x.astype(BF16)) + rb_ref[...]
    e_iota = lax.broadcasted_iota(jnp.int32, (N_EXPERTS, tm), 0)
    m1 = jnp.max(lg, axis=0, keepdims=True)
    i1 = jnp.min(jnp.where(lg == m1, e_iota, N_EXPERTS), axis=0, keepdims=True)
    lg2 = jnp.where(e_iota == i1, NEG_INF, lg)
    m2 = jnp.max(lg2, axis=0, keepdims=True)
    i2 = jnp.min(jnp.where(lg2 == m2, e_iota, N_EXPERTS), axis=0, keepdims=True)
    e2 = jnp.exp(m2 - m1)
    den = 1.0 + e2
    w1 = 1.0 / den
    w2 = e2 / den
    sel1 = e_iota == i1
    sel2 = e_iota == i2
    oh = jnp.where(sel1 | sel2, 1.0, 0.0)
    pre = _dot(oh.astype(BF16), tri_ref[...]) + carry[:, 0:1]
    r1 = jnp.sum(jnp.where(sel1, pre, 0.0), axis=0, keepdims=True)
    r2 = jnp.sum(jnp.where(sel2, pre, 0.0), axis=0, keepdims=True)
    carry[...] = carry[...] + jnp.sum(oh, axis=1, keepdims=True)
    rows = [i1.astype(F32), i2.astype(F32), r1, r2, w1, w2]
    slab = jnp.zeros((N_EXPERTS, tm), F32)
    for ri, rv in enumerate(rows):
        slab = jnp.where(e_iota == ri, rv, slab)
    slab_ref[...] = slab
    cnt_ref[...] = carry[...]


def _route(xa, xb, wrt, rb, *, tm):
    na, nb_ = xa.shape[0] // tm, xb.shape[0] // tm
    tri = jnp.triu(jnp.ones((tm, tm), BF16), k=1)
    spec_a, spec_b = _two_group_specs(tm, na, nb_)
    return pl.pallas_call(
        functools.partial(_route_kernel, tm=tm, n_a_tiles=na),
        grid=(na + nb_,),
        in_specs=[spec_a, spec_b,
                  _const_spec((N_EXPERTS, D_MODEL)), _const_spec((N_EXPERTS, 1)), _const_spec((tm, tm))],
        out_specs=(pl.BlockSpec((N_EXPERTS, tm), lambda i: (0, i)),
                   pl.BlockSpec((N_EXPERTS, LANES), lambda i: (0, 0))),
        out_shape=(jax.ShapeDtypeStruct((N_EXPERTS, (na + nb_) * tm), F32),
                   jax.ShapeDtypeStruct((N_EXPERTS, LANES), F32)),
        scratch_shapes=[pltpu.VMEM((N_EXPERTS, LANES), F32)],
        compiler_params=pltpu.CompilerParams(dimension_semantics=("arbitrary",)),
        name="moe_route",
    )(xa, xb, wrt, rb, tri)


def _row_copy(src, src_row, dst, dst_row, sem):
    return pltpu.make_async_copy(src.at[pl.ds(src_row, 1)], dst.at[pl.ds(dst_row, 1)], sem)


def _dispatch_kernel(slot_ref, xa_ref, xb_ref, z_ref, xs_ref, sem, *, tm, n_a_tiles, n_b_tiles):
    i = pl.program_id(0)

    def issue_from(src_ref):
        def issue(r, c):
            _row_copy(src_ref, r, xs_ref, slot_ref[0, 0, r], sem).start(priority=0)
            _row_copy(src_ref, r, xs_ref, slot_ref[0, 0, tm + r], sem).start(priority=1)
            return c

        lax.fori_loop(0, tm, issue, 0, unroll=8)

    @pl.when(i < n_a_tiles)
    def _():
        issue_from(xa_ref)

    @pl.when((i >= n_a_tiles) & (i < n_a_tiles + n_b_tiles))
    def _():
        issue_from(xb_ref)

    @pl.when(i >= n_a_tiles + n_b_tiles)
    def _():
        issue_from(z_ref)

    for _ in range(2):
        pltpu.make_async_copy(z_ref, xs_ref.at[pl.ds(0, tm)], sem).wait()


def _dispatch(xa, xb, slots3, n_rows, *, tm):
    na, nb_ = xa.shape[0] // tm, xb.shape[0] // tm
    nz = slots3.shape[0] - na - nb_
    spec_a, spec_b = _two_group_specs(tm, na, nb_)
    zsrc = jnp.zeros((tm, D_MODEL), F32)
    return pl.pallas_call(
        functools.partial(_dispatch_kernel, tm=tm, n_a_tiles=na, n_b_tiles=nb_),
        grid=(na + nb_ + nz,),
        in_specs=[pl.BlockSpec((1, 1, 2 * tm), lambda i: (i, 0, 0), memory_space=pltpu.SMEM),
                  spec_a, spec_b, _const_spec((tm, D_MODEL))],
        out_specs=pl.BlockSpec(memory_space=pl.ANY),
        out_shape=jax.ShapeDtypeStruct((n_rows, D_MODEL), F32),
        scratch_shapes=[pltpu.SemaphoreType.DMA(())],
        compiler_params=pltpu.CompilerParams(dimension_semantics=("arbitrary",)),
        name="moe_dispatch",
    )(slots3, xa, xb, zsrc)


def _expert_kernel(te_ref, nt_ref, xs_ref, wg_ref, wu_ref, wd_ref, o_ref):
    del te_ref
    live = pl.program_id(0) < nt_ref[0]

    @pl.when(live)
    def _():
        o_ref[...] = _swiglu(xs_ref[...].astype(BF16), wg_ref[0], wu_ref[0], wd_ref[0])

    @pl.when(jnp.logical_not(live))
    def _():
        o_ref[...] = jnp.zeros_like(o_ref)


def _experts(xs, tile_expert, n_tiles, wg, wu, wd, *, tm):
    n_rows = xs.shape[0]

    def row_map(i, te, nt):
        return (jnp.minimum(i, nt[0] - 1), 0)

    def w_map(i, te, nt):
        return (te[i], 0, 0)

    return pl.pallas_call(
        _expert_kernel,
        grid_spec=pltpu.PrefetchScalarGridSpec(
            num_scalar_prefetch=2,
            grid=(n_rows // tm,),
            in_specs=[pl.BlockSpec((tm, D_MODEL), row_map),
                      pl.BlockSpec((1, D_MODEL, D_FF), w_map),
                      pl.BlockSpec((1, D_MODEL, D_FF), w_map),
                      pl.BlockSpec((1, D_FF, D_MODEL), w_map)],
            out_specs=pl.BlockSpec((tm, D_MODEL), lambda i, te, nt: (i, 0)),
        ),
        out_shape=jax.ShapeDtypeStruct((n_rows, D_MODEL), F32),
        compiler_params=pltpu.CompilerParams(dimension_semantics=("arbitrary",), vmem_limit_bytes=VMEM_LIMIT),
        name="moe_experts",
    )(tile_expert, n_tiles, xs, wg, wu, wd)


def _combine_kernel(slot_ref, x_ref, slab_ref, ys_ref, g_ref, b_ref, o_ref, buf0, buf1, sem, *, tm, alpha):
    def issue(r, c):
        _row_copy(ys_ref, slot_ref[0, 0, r], buf0, r, sem).start(priority=0)
        _row_copy(ys_ref, slot_ref[0, 0, tm + r], buf1, r, sem).start(priority=1)
        return c

    lax.fori_loop(0, tm, issue, 0, unroll=8)

    pltpu.make_async_copy(ys_ref.at[pl.ds(0, tm)], buf0, sem).wait()
    pltpu.make_async_copy(ys_ref.at[pl.ds(0, tm)], buf1, sem).wait()
    slab_t = jnp.concatenate([slab_ref[...], jnp.zeros((LANES - N_EXPERTS, tm), F32)], axis=0).T
    y = slab_t[:, 4:5] * buf0[...] + slab_t[:, 5:6] * buf1[...]
    o_ref[...] = _layer_norm(alpha * x_ref[...] + y, g_ref[...], b_ref[...])


def _combine(x, slots3, slab, ys, g, b, *, tm, tile0, alpha, name):
    n = x.shape[0]
    return pl.pallas_call(
        functools.partial(_combine_kernel, tm=tm, alpha=alpha),
        grid=(n // tm,),
        in_specs=[pl.BlockSpec((1, 1, 2 * tm), lambda i: (i + tile0, 0, 0), memory_space=pltpu.SMEM),
                  pl.BlockSpec((tm, D_MODEL), lambda i: (i, 0)),
                  pl.BlockSpec((N_EXPERTS, tm), lambda i: (0, i + tile0)),
                  pl.BlockSpec(memory_space=pl.ANY),
                  _const_spec((1, D_MODEL)), _const_spec((1, D_MODEL))],
        out_specs=pl.BlockSpec((tm, D_MODEL), lambda i: (i, 0)),
        out_shape=jax.ShapeDtypeStruct((n, D_MODEL), F32),
        scratch_shapes=[pltpu.VMEM((tm, D_MODEL), F32), pltpu.VMEM((tm, D_MODEL), F32),
                        pltpu.SemaphoreType.DMA(())],
        compiler_params=pltpu.CompilerParams(dimension_semantics=("arbitrary",)),
        name=name,
    )(slots3, x, slab, ys, g, b)


def _free_slots(counts, offs, padded, n_rows, n_free):
    starts = jnp.concatenate([offs + counts, (offs[-1:] + padded[-1:])])
    gaps = jnp.concatenate([padded - counts, n_rows - (offs[-1:] + padded[-1:])])
    cum = jnp.cumsum(gaps)
    prev = cum - gaps
    idx = jnp.arange(n_free, dtype=jnp.int32)
    seg = jnp.sum(idx[:, None] >= cum[None, :], axis=1)
    pick = seg[:, None] == jnp.arange(N_EXPERTS + 1, dtype=jnp.int32)[None, :]
    return jnp.sum(jnp.where(pick, (starts - prev)[None, :], 0), axis=1) + idx


def _moe(xa, xb, router_w, router_b, wg, wu, wd, g, b, *, alpha, tm_rows, tm_expert):
    na, nb_ = xa.shape[0], xb.shape[0]
    n = na + nb_
    slab, cnt = _route(xa, xb, router_w.T.astype(BF16), router_b.reshape(N_EXPERTS, 1), tm=tm_rows)
    counts = cnt[:, 0].astype(jnp.int32)
    tiles_per = (counts + tm_expert - 1) // tm_expert
    padded = tiles_per * tm_expert
    ends = jnp.cumsum(tiles_per)
    offs = (ends - tiles_per) * tm_expert
    i1 = slab[0].astype(jnp.int32)
    i2 = slab[1].astype(jnp.int32)
    eye = jnp.arange(N_EXPERTS, dtype=jnp.int32)[:, None]
    off1 = jnp.sum(jnp.where(eye == i1[None, :], offs[:, None], 0), axis=0)
    off2 = jnp.sum(jnp.where(eye == i2[None, :], offs[:, None], 0), axis=0)
    slots = jnp.concatenate([(off1 + slab[2].astype(jnp.int32)).reshape(-1, 1, tm_rows),
                             (off2 + slab[3].astype(jnp.int32)).reshape(-1, 1, tm_rows)], axis=2)
    n_free = N_EXPERTS * tm_expert
    n_rows = 2 * n + n_free
    assert n_free % (2 * tm_rows) == 0
    free = _free_slots(counts, offs, padded, n_rows, n_free)
    max_tiles = n_rows // tm_expert
    tile_expert = jnp.minimum(
        jnp.sum(jnp.arange(max_tiles, dtype=jnp.int32)[:, None] >= ends[None, :], axis=1), N_EXPERTS - 1
    ).astype(jnp.int32)
    n_tiles = ends[-1:].astype(jnp.int32)
    slots3 = jnp.concatenate([slots, free.reshape(-1, 1, 2 * tm_rows)], axis=0)

    xs = _dispatch(xa, xb, slots3, n_rows, tm=tm_rows)
    ys = _experts(xs, tile_expert, n_tiles, wg, wu, wd, tm=tm_expert)
    ya = _combine(xa, slots3, slab, ys, g, b, tm=tm_rows, tile0=0, alpha=alpha, name="moe_combine_a")
    yb = _combine(xb, slots3, slab, ys, g, b, tm=tm_rows, tile0=na // tm_rows, alpha=alpha, name="moe_combine_b")
    return ya, yb


def _layer_weights(l, w_in, conv_w, sink, p_attn, p_conv, p_mem, w_o, w_mem_k, w_mem_v, ln1_g, ln1_b):
    return dict(
        sink=sink[l],
        w_in=w_in[l].astype(BF16),
        conv_w=conv_w[l],
        p_attn=p_attn[l].astype(BF16),
        p_conv=p_conv[l].astype(BF16),
        p_mem=p_mem[l].astype(BF16),
        w_o=w_o[l].astype(BF16),
        w_mem_k=w_mem_k[l].astype(BF16),
        w_mem_v=w_mem_v[l].astype(BF16),
        ln1_g=ln1_g[l].reshape(1, D_MODEL),
        ln1_b=ln1_b[l].reshape(1, D_MODEL),
    )


def _feature_major(c):
    d, nb, tok, h, hd = c.shape
    return jnp.transpose(c, (0, 1, 3, 4, 2)).reshape(d, nb, h * hd, tok)


def _token_major(c, heads):
    d, nb, f, tok = c.shape
    return jnp.transpose(c.reshape(d, nb, heads, f // heads, tok), (0, 1, 4, 2, 3))


def kernel(x_prompt, x_sample, mem_prompt, cache_win_k, cache_win_v, cache_conv, cache_mem_k, cache_mem_v, w_in, conv_w, sink, p_attn, p_conv, p_mem, w_o, w_mem_k, w_mem_v, ln1_g, ln1_b, ln2_g, ln2_b, ffn_w_gate, ffn_w_up, ffn_w_down, router_w, router_b, exp_w_gate, exp_w_up, exp_w_down):
    depth = w_in.shape[0]
    alpha = (2 * depth) ** 0.25
    b, seq, _ = x_prompt.shape
    nb, t, _ = x_sample.shape
    n_p = b * seq
    n_s = nb * t

    xp = x_prompt
    xs = x_sample.reshape(n_s, D_MODEL)
    ckt = _feature_major(cache_win_k)
    cvt = _feature_major(cache_win_v)
    cmkt = _feature_major(cache_mem_k)
    cmvt = _feature_major(cache_mem_v)

    outs = {k_: [] for k_ in ("wk_p", "wv_p", "cv_p", "mk_p", "mv_p", "wk_s", "wv_s", "cv_s")}
    for l in range(depth):
        w = _layer_weights(l, w_in, conv_w, sink, p_attn, p_conv, p_mem, w_o, w_mem_k, w_mem_v, ln1_g, ln1_b)
        x1p, wk, wv, cvp, mk, mv = _mixer_prompt(xp, mem_prompt, w, l, ts=1024, alpha=alpha)
        x1s, wks, wvs, cvs = _mixer_sample(xs, ckt, cvt, cache_conv, cmkt, cmvt, w, l, bb=LANES // t, t=t, alpha=alpha)
        outs["wk_p"].append(wk); outs["wv_p"].append(wv); outs["cv_p"].append(cvp)
        outs["mk_p"].append(mk); outs["mv_p"].append(mv)
        outs["wk_s"].append(wks); outs["wv_s"].append(wvs); outs["cv_s"].append(cvs)
        x1p = x1p.reshape(n_p, D_MODEL)
        g2 = ln2_g[l].reshape(1, D_MODEL)
        b2 = ln2_b[l].reshape(1, D_MODEL)
        i = l // 2
        if l % 2 == 0:
            wg, wu, wd = ffn_w_gate[i].astype(BF16), ffn_w_up[i].astype(BF16), ffn_w_down[i].astype(BF16)
            x2p = _ffn_dense(x1p, wg, wu, wd, g2, b2, tm=512, alpha=alpha, name=f"ffn_dense_p_l{l}")
            xs = _ffn_dense(x1s, wg, wu, wd, g2, b2, tm=512, alpha=alpha, name=f"ffn_dense_s_l{l}")
        else:
            x2p, xs = _moe(x1p, x1s, router_w[i], router_b[i], exp_w_gate[i].astype(BF16), exp_w_up[i].astype(BF16),
                           exp_w_down[i].astype(BF16), g2, b2, alpha=alpha, tm_rows=512, tm_expert=256)
        xp = x2p.reshape(b, seq, D_MODEL)

    return (xp, xs.reshape(nb, t, D_MODEL),
            _token_major(jnp.stack(outs["wk_p"]), N_KV_HEADS),
            _token_major(jnp.stack(outs["wv_p"]), N_KV_HEADS),
            jnp.stack(outs["cv_p"]),
            _token_major(jnp.stack(outs["mk_p"]), MEM_HEADS),
            _token_major(jnp.stack(outs["mv_p"]), MEM_HEADS),
            _token_major(jnp.stack(outs["wk_s"]), N_KV_HEADS),
            _token_major(jnp.stack(outs["wv_s"]), N_KV_HEADS),
            jnp.stack(outs["cv_s"]))
```

```python
import functools

import jax
import jax.numpy as jnp
from jax import lax
from jax.experimental import pallas as pl
from jax.experimental.pallas import tpu as pltpu

F32 = jnp.float32
BF16 = jnp.bfloat16

D_MODEL = 1024
HEAD_DIM = 64
N_HEADS = 8
N_KV_HEADS = 2
GROUP = N_HEADS // N_KV_HEADS
ATTN_DIM = N_HEADS * HEAD_DIM
KV_DIM = N_KV_HEADS * HEAD_DIM
WINDOW = 128
ATTN_SCALE = HEAD_DIM ** -0.5
CONV_DIM = 256
CONV_WIDTH = 3
N_MEM = 256
MEM_HEADS = 4
MEM_DIM = MEM_HEADS * HEAD_DIM
D_FF = 2816
N_EXPERTS = 8
LN_EPS = 1e-5

_Q0, _K0, _V0, _H0, _GB0, _GC0, _QM0, _GA0 = 0, 512, 640, 768, 1024, 1280, 1536, 1792
_MIX_COLS = _GA0
IN_DIM = _GA0 + 3 * D_MODEL

LANES = 128
VMEM_LIMIT = 56 * 1024 * 1024

NEG_INF = float("-inf")


def _dup_kv_lanes(kv):
    lo = lax.broadcasted_iota(jnp.int32, kv.shape, 1) < HEAD_DIM
    swapped = pltpu.roll(kv, HEAD_DIM, 1)
    return [jnp.where(lo, kv, swapped).astype(BF16), jnp.where(lo, swapped, kv).astype(BF16)]


def _layer_norm(v, g, b):
    mu = jnp.mean(v, axis=-1, keepdims=True)
    c = v - mu
    var = jnp.mean(c * c, axis=-1, keepdims=True)
    return c * lax.rsqrt(var + LN_EPS) * g + b


def _dot(a, b):
    return jnp.dot(a, b, preferred_element_type=F32)


def _dot_nt(a, b):
    return lax.dot_general(a, b, (((1,), (1,)), ((), ())), preferred_element_type=F32)


def _stack_lane_halves(q, n_blocks):
    m = q.shape[0]
    lo = lax.broadcasted_iota(jnp.int32, (m, LANES), 1) < HEAD_DIM
    parts = []
    for j in range(n_blocks):
        blk = q[:, j * LANES:(j + 1) * LANES]
        parts.append(jnp.where(lo, blk, 0.0))
        parts.append(jnp.where(lo, 0.0, blk))
    return jnp.concatenate(parts, axis=0)


def _unstack_lane_halves(r, m, n_blocks):
    lo = lax.broadcasted_iota(jnp.int32, (m, LANES), 1) < HEAD_DIM
    cols = [jnp.where(lo, r[(2 * j) * m:(2 * j + 1) * m], r[(2 * j + 1) * m:(2 * j + 2) * m])
            for j in range(n_blocks)]
    return jnp.concatenate(cols, axis=1)


def _merge_project_norm(x, zg, o_a, o_c, o_m, p_attn_ref, p_conv_ref, p_mem_ref, w_o_ref, g_ref, b_ref, alpha):
    merged = (jax.nn.sigmoid(zg[:, 0:D_MODEL]) * _dot(o_a.astype(BF16), p_attn_ref[...])
              + jax.nn.sigmoid(zg[:, D_MODEL:2 * D_MODEL]) * _dot(o_c.astype(BF16), p_conv_ref[...])
              + jax.nn.sigmoid(zg[:, 2 * D_MODEL:3 * D_MODEL]) * _dot(o_m.astype(BF16), p_mem_ref[...]))
    mix = _dot(merged.astype(BF16), w_o_ref[...])
    return _layer_norm(alpha * x + mix, g_ref[...], b_ref[...])


def _const_spec(shape):
    n = len(shape)
    return pl.BlockSpec(shape, lambda *_: (0,) * n, pipeline_mode=pl.Buffered(1))


def _mixer_prompt_kernel(sink_ref, x_ref, mem_ref, w_in_ref, convw_ref, p_attn_ref, p_conv_ref, p_mem_ref,
                         w_o_ref, wmk_ref, wmv_ref, g_ref, b_ref,
                         x1_ref, wk_ref, wv_ref, cv_ref, mk_ref, mv_ref,
                         kprev, vprev, uprev, mk_s, mv_s, *, ts, alpha):
    s = pl.program_id(1)
    ns = pl.num_programs(1)
    x = x_ref[0]
    xb = x.astype(BF16)

    @pl.when(s == 0)
    def _():
        memb = mem_ref[0].astype(BF16)
        mk = _dot(memb, wmk_ref[...])
        mv = _dot(memb, wmv_ref[...])
        mk_ref[0] = mk.T
        mv_ref[0] = mv.T
        mk_s[...] = mk.astype(BF16)
        mv_s[...] = mv.astype(BF16)
        kprev[...] = jnp.zeros_like(kprev)
        vprev[...] = jnp.zeros_like(vprev)
        uprev[...] = jnp.zeros_like(uprev)

    z = _dot(xb, w_in_ref[:, 0:_MIX_COLS])
    q = z[:, _Q0:_K0] * ATTN_SCALE
    k = z[:, _K0:_V0]
    v = z[:, _V0:_H0]
    hc = z[:, _H0:_GB0]
    g_b = z[:, _GB0:_GC0]
    g_c = z[:, _GC0:_QM0]
    qm = z[:, _QM0:_GA0] * ATTN_SCALE

    k_ext = jnp.concatenate([kprev[...], k], axis=0)
    v_ext = jnp.concatenate([vprev[...], v], axis=0)
    kd = _dup_kv_lanes(k_ext)
    vd = _dup_kv_lanes(v_ext)
    half = GROUP * WINDOW
    row_i = lax.broadcasted_iota(jnp.int32, (WINDOW, 2 * WINDOW), 0)
    col_j = lax.broadcasted_iota(jnp.int32, (WINDOW, 2 * WINDOW), 1)
    band = (col_j > row_i) & (col_j <= row_i + WINDOW)
    band_first = band & ((col_j >= WINDOW) | (s > 0))
    o_blocks = []
    for blk in range(ts // WINDOW):
        qs = _stack_lane_halves(q[blk * WINDOW:(blk + 1) * WINDOW], GROUP).astype(BF16)
        band_rows = slice(blk * WINDOW, blk * WINDOW + 2 * WINDOW)
        sc = jnp.concatenate([_dot_nt(qs[gi * half:(gi + 1) * half], kd[gi][band_rows])
                              for gi in range(N_KV_HEADS)], axis=0)
        mask = band_first if blk == 0 else band
        ps = []
        for hh in range(N_HEADS):
            sink_h = sink_ref[hh]
            sch = jnp.where(mask, sc[hh * WINDOW:(hh + 1) * WINDOW], NEG_INF)
            m = jnp.maximum(jnp.max(sch, axis=-1, keepdims=True), sink_h)
            e = jnp.exp(sch - m)
            den = jnp.sum(e, axis=-1, keepdims=True) + jnp.exp(sink_h - m)
            ps.append((e / den).astype(BF16))
        r = jnp.concatenate([_dot(jnp.concatenate(ps[gi * GROUP:(gi + 1) * GROUP], axis=0), vd[gi][band_rows])
                             for gi in range(N_KV_HEADS)], axis=0)
        o_blocks.append(_unstack_lane_halves(r, WINDOW, GROUP))
    o_a = jnp.concatenate(o_blocks, axis=0)
    kprev[...] = k_ext[ts:ts + WINDOW]
    vprev[...] = v_ext[ts:ts + WINDOW]

    u = g_c * hc
    rows = lax.broadcasted_iota(jnp.int32, (ts, CONV_DIM), 0)
    up1 = uprev[7:8, :]
    up2 = uprev[6:7, :]
    u_m1 = jnp.where(rows == 0, up1, pltpu.roll(u, 1, 0))
    u_m2 = jnp.where(rows == 0, up2, jnp.where(rows == 1, up1, pltpu.roll(u, 2, 0)))
    cw = convw_ref[...]
    conv = cw[0:1, :] * u_m2 + cw[1:2, :] * u_m1 + cw[2:3, :] * u
    o_c = g_b * conv
    uprev[...] = u[ts - 8:ts]

    qms = _stack_lane_halves(qm, MEM_HEADS // 2).astype(BF16)
    o_cols = []
    lo = lax.broadcasted_iota(jnp.int32, (ts, LANES), 1) < HEAD_DIM
    for jb in range(MEM_HEADS // 2):
        scm = _dot_nt(qms[2 * jb * ts:(2 * jb + 2) * ts], mk_s[:, jb * LANES:(jb + 1) * LANES])
        mm = jnp.max(scm, axis=-1, keepdims=True)
        em = jnp.exp(scm - mm)
        pm = (em / jnp.sum(em, axis=-1, keepdims=True)).astype(BF16)
        rm = _dot(pm, mv_s[:, jb * LANES:(jb + 1) * LANES])
        o_cols.append(jnp.where(lo, rm[0:ts], rm[ts:2 * ts]))
    o_m = jnp.concatenate(o_cols, axis=1)

    zg = _dot(xb, w_in_ref[:, _GA0:IN_DIM])
    x1_ref[0] = _merge_project_norm(x, zg, o_a, o_c, o_m, p_attn_ref, p_conv_ref, p_mem_ref, w_o_ref,
                                    g_ref, b_ref, alpha)

    @pl.when(s == ns - 1)
    def _():
        wk_ref[0] = k[ts - WINDOW:ts].T
        wv_ref[0] = v[ts - WINDOW:ts].T
        cv_ref[0] = u[ts - (CONV_WIDTH - 1):ts]


def _mixer_prompt(x, mem, w, l, *, ts, alpha):
    b, seq, _ = x.shape
    kern = functools.partial(_mixer_prompt_kernel, ts=ts, alpha=alpha)
    smem = pl.BlockSpec(memory_space=pltpu.SMEM)
    in_specs = [
        smem,
        pl.BlockSpec((1, ts, D_MODEL), lambda i, s: (i, s, 0)),
        pl.BlockSpec((1, N_MEM, D_MODEL), lambda i, s: (i, 0, 0)),
        _const_spec((D_MODEL, IN_DIM)),
        _const_spec((CONV_WIDTH, CONV_DIM)),
        _const_spec((ATTN_DIM, D_MODEL)),
        _const_spec((CONV_DIM, D_MODEL)),
        _const_spec((MEM_DIM, D_MODEL)),
        _const_spec((D_MODEL, D_MODEL)),
        _const_spec((D_MODEL, MEM_DIM)),
        _const_spec((D_MODEL, MEM_DIM)),
        _const_spec((1, D_MODEL)),
        _const_spec((1, D_MODEL)),
    ]
    out_shape = (
        jax.ShapeDtypeStruct((b, seq, D_MODEL), F32),
        jax.ShapeDtypeStruct((b, KV_DIM, WINDOW), F32),
        jax.ShapeDtypeStruct((b, KV_DIM, WINDOW), F32),
        jax.ShapeDtypeStruct((b, CONV_WIDTH - 1, CONV_DIM), F32),
        jax.ShapeDtypeStruct((b, MEM_DIM, N_MEM), F32),
        jax.ShapeDtypeStruct((b, MEM_DIM, N_MEM), F32),
    )
    out_specs = (
        pl.BlockSpec((1, ts, D_MODEL), lambda i, s: (i, s, 0)),
        pl.BlockSpec((1, KV_DIM, WINDOW), lambda i, s: (i, 0, 0)),
        pl.BlockSpec((1, KV_DIM, WINDOW), lambda i, s: (i, 0, 0)),
        pl.BlockSpec((1, CONV_WIDTH - 1, CONV_DIM), lambda i, s: (i, 0, 0)),
        pl.BlockSpec((1, MEM_DIM, N_MEM), lambda i, s: (i, 0, 0)),
        pl.BlockSpec((1, MEM_DIM, N_MEM), lambda i, s: (i, 0, 0)),
    )
    scratch = [
        pltpu.VMEM((WINDOW, KV_DIM), F32),
        pltpu.VMEM((WINDOW, KV_DIM), F32),
        pltpu.VMEM((8, CONV_DIM), F32),
        pltpu.VMEM((N_MEM, MEM_DIM), BF16),
        pltpu.VMEM((N_MEM, MEM_DIM), BF16),
    ]
    return pl.pallas_call(
        kern,
        grid=(b, seq // ts),
        in_specs=in_specs,
        out_specs=out_specs,
        out_shape=out_shape,
        scratch_shapes=scratch,
        compiler_params=pltpu.CompilerParams(dimension_semantics=("arbitrary", "arbitrary"),
                                             vmem_limit_bytes=VMEM_LIMIT),
        name=f"mixer_prompt_l{l}",
    )(w["sink"], x, mem, w["w_in"], w["conv_w"], w["p_attn"], w["p_conv"], w["p_mem"], w["w_o"],
      w["w_mem_k"], w["w_mem_v"], w["ln1_g"], w["ln1_b"])


def _mixer_sample_kernel(sink_ref, x_ref, ck_ref, cvv_ref, cc_ref, cmk_ref, cmv_ref, w_in_ref, convw_ref,
                         p_attn_ref, p_conv_ref, p_mem_ref, w_o_ref, g_ref, b_ref,
                         x1_ref, wk_ref, wv_ref, cv_ref, *, bb, t, alpha):
    nbuf = ck_ref.shape[2]
    x = x_ref[...]
    xb = x.astype(BF16)
    z = _dot(xb, w_in_ref[:, 0:_MIX_COLS])
    q = z[:, _Q0:_K0] * ATTN_SCALE
    u = z[:, _GC0:_QM0] * z[:, _H0:_GB0]
    g_b = z[:, _GB0:_GC0]
    qm = z[:, _QM0:_GA0] * ATTN_SCALE
    cw = convw_ref[...]
    kt_new = z[:, _K0:_V0].T
    vt_new = z[:, _V0:_H0].T

    def dup_rows(kv_t):
        kvb = kv_t.astype(BF16)
        return [jnp.concatenate([kvb[gi * HEAD_DIM:(gi + 1) * HEAD_DIM]] * 2, axis=0) for gi in range(N_KV_HEADS)]

    nstk = N_HEADS * t
    r_t = lax.broadcasted_iota(jnp.int32, (nstk, 2 * WINDOW), 0) % t
    c_j = lax.broadcasted_iota(jnp.int32, (nstk, 2 * WINDOW), 1)
    dist = r_t + nbuf - c_j
    wmask = (dist >= 0) & (dist < WINDOW) & (c_j < nbuf + t)
    blk_id = lax.broadcasted_iota(jnp.int32, (nstk, 1), 0) // t
    sink_col = jnp.zeros((nstk, 1), F32)
    for hh in range(N_HEADS):
        sink_col = jnp.where(blk_id == hh, sink_ref[hh], sink_col)
    mlane = lax.broadcasted_iota(jnp.int32, (t, MEM_DIM), 1) // HEAD_DIM
    klane = lax.broadcasted_iota(jnp.int32, (KV_DIM, LANES), 1)

    o_a_rows, o_c_rows, o_m_rows = [], [], []
    for bi in range(bb):
        sl = slice(bi * t, (bi + 1) * t)
        kc = ck_ref[bi]
        vc = cvv_ref[bi]
        k_new0 = jnp.where(klane < t, pltpu.roll(kt_new, (LANES - bi * t) % LANES, 1), 0.0)
        v_new0 = jnp.where(klane < t, pltpu.roll(vt_new, (LANES - bi * t) % LANES, 1), 0.0)
        kd = dup_rows(jnp.concatenate([kc, k_new0], axis=1))
        vd = dup_rows(jnp.concatenate([vc, v_new0], axis=1))
        qs = _stack_lane_halves(q[sl], GROUP).astype(BF16)
        hrows = GROUP * t
        sc = jnp.concatenate([_dot(qs[gi * hrows:(gi + 1) * hrows], kd[gi]) for gi in range(N_KV_HEADS)], axis=0)
        sc = jnp.where(wmask, sc, NEG_INF)
        m = jnp.maximum(jnp.max(sc, axis=-1, keepdims=True), sink_col)
        e = jnp.exp(sc - m)
        den = jnp.sum(e, axis=-1, keepdims=True) + jnp.exp(sink_col - m)
        p = (e / den).astype(BF16)
        r = jnp.concatenate([_dot_nt(p[gi * hrows:(gi + 1) * hrows], vd[gi]) for gi in range(N_KV_HEADS)], axis=0)
        o_a_rows.append(_unstack_lane_halves(r, t, GROUP))
        tail = (nbuf - t - bi * t) % LANES
        wk_ref[bi] = jnp.where(klane < nbuf - t, pltpu.roll(kc, nbuf - t, 1), pltpu.roll(kt_new, tail, 1))
        wv_ref[bi] = jnp.where(klane < nbuf - t, pltpu.roll(vc, nbuf - t, 1), pltpu.roll(vt_new, tail, 1))
        u_ext = jnp.concatenate([cc_ref[bi], u[sl]], axis=0)
        conv = cw[0:1, :] * u_ext[0:t] + cw[1:2, :] * u_ext[1:t + 1] + cw[2:3, :] * u_ext[2:t + 2]
        o_c_rows.append(g_b[sl] * conv)
        cv_ref[bi] = u_ext[t:t + CONV_WIDTH - 1]
        qmb = qm[sl]
        qms = jnp.concatenate([jnp.where(mlane == hh, qmb, 0.0) for hh in range(MEM_HEADS)], axis=0).astype(BF16)
        scm = _dot(qms, cmk_ref[bi].astype(BF16))
        mm = jnp.max(scm, axis=-1, keepdims=True)
        em = jnp.exp(scm - mm)
        pm = (em / jnp.sum(em, axis=-1, keepdims=True)).astype(BF16)
        rm = _dot_nt(pm, cmv_ref[bi].astype(BF16))
        om = jnp.zeros((t, MEM_DIM), F32)
        for hh in range(MEM_HEADS):
            om = jnp.where(mlane == hh, rm[hh * t:(hh + 1) * t], om)
        o_m_rows.append(om)

    o_a = jnp.concatenate(o_a_rows, axis=0)
    o_c = jnp.concatenate(o_c_rows, axis=0)
    o_m = jnp.concatenate(o_m_rows, axis=0)
    zg = _dot(xb, w_in_ref[:, _GA0:IN_DIM])
    x1_ref[...] = _merge_project_norm(x, zg, o_a, o_c, o_m, p_attn_ref, p_conv_ref, p_mem_ref, w_o_ref,
                                      g_ref, b_ref, alpha)


def _mixer_sample(x2d, ckt, cvt, cc, cmkt, cmvt, w, l, *, bb, t, alpha):
    nb, nbuf = ckt.shape[1], ckt.shape[3]
    assert nbuf == LANES and bb * t == LANES and t % 8 == 0
    kern = functools.partial(_mixer_sample_kernel, bb=bb, t=t, alpha=alpha)
    smem = pl.BlockSpec(memory_space=pltpu.SMEM)

    def cache_spec(d1, d2):
        return pl.BlockSpec((None, bb, d1, d2), lambda i: (l, i, 0, 0))

    in_specs = [
        smem,
        pl.BlockSpec((bb * t, D_MODEL), lambda i: (i, 0)),
        cache_spec(KV_DIM, nbuf),
        cache_spec(KV_DIM, nbuf),
        cache_spec(CONV_WIDTH - 1, CONV_DIM),
        cache_spec(MEM_DIM, N_MEM),
        cache_spec(MEM_DIM, N_MEM),
        _const_spec((D_MODEL, IN_DIM)),
        _const_spec((CONV_WIDTH, CONV_DIM)),
        _const_spec((ATTN_DIM, D_MODEL)),
        _const_spec((CONV_DIM, D_MODEL)),
        _const_spec((MEM_DIM, D_MODEL)),
        _const_spec((D_MODEL, D_MODEL)),
        _const_spec((1, D_MODEL)),
        _const_spec((1, D_MODEL)),
    ]
    out_shape = (
        jax.ShapeDtypeStruct((nb * t, D_MODEL), F32),
        jax.ShapeDtypeStruct((nb, KV_DIM, nbuf), F32),
        jax.ShapeDtypeStruct((nb, KV_DIM, nbuf), F32),
        jax.ShapeDtypeStruct((nb, CONV_WIDTH - 1, CONV_DIM), F32),
    )
    out_specs = (
        pl.BlockSpec((bb * t, D_MODEL), lambda i: (i, 0)),
        pl.BlockSpec((bb, KV_DIM, nbuf), lambda i: (i, 0, 0)),
        pl.BlockSpec((bb, KV_DIM, nbuf), lambda i: (i, 0, 0)),
        pl.BlockSpec((bb, CONV_WIDTH - 1, CONV_DIM), lambda i: (i, 0, 0)),
    )
    return pl.pallas_call(
        kern,
        grid=(nb // bb,),
        in_specs=in_specs,
        out_specs=out_specs,
        out_shape=out_shape,
        compiler_params=pltpu.CompilerParams(dimension_semantics=("arbitrary",),
                                             vmem_limit_bytes=VMEM_LIMIT),
        name=f"mixer_sample_l{l}",
    )(w["sink"], x2d, ckt, cvt, cc, cmkt, cmvt, w["w_in"], w["conv_w"], w["p_attn"], w["p_conv"],
      w["p_mem"], w["w_o"], w["ln1_g"], w["ln1_b"])


def _swiglu(xb, wg, wu, wd):
    g = _dot(xb, wg)
    u = _dot(xb, wu)
    return _dot((jax.nn.silu(g) * u).astype(BF16), wd)


def _ffn_kernel(x_ref, wg_ref, wu_ref, wd_ref, g_ref, b_ref, o_ref, *, alpha):
    x = x_ref[...]
    y = _swiglu(x.astype(BF16), wg_ref[...], wu_ref[...], wd_ref[...])
    o_ref[...] = _layer_norm(alpha * x + y, g_ref[...], b_ref[...])


def _ffn_dense(x, wg, wu, wd, g, b, *, tm, alpha, name):
    n = x.shape[0]
    return pl.pallas_call(
        functools.partial(_ffn_kernel, alpha=alpha),
        grid=(n // tm,),
        in_specs=[pl.BlockSpec((tm, D_MODEL), lambda i: (i, 0)),
                  _const_spec((D_MODEL, D_FF)), _const_spec((D_MODEL, D_FF)), _const_spec((D_FF, D_MODEL)),
                  _const_spec((1, D_MODEL)), _const_spec((1, D_MODEL))],
        out_specs=pl.BlockSpec((tm, D_MODEL), lambda i: (i, 0)),
        out_shape=jax.ShapeDtypeStruct((n, D_MODEL), F32),
        compiler_params=pltpu.CompilerParams(dimension_semantics=("arbitrary",), vmem_limit_bytes=VMEM_LIMIT),
        name=name,
    )(x, wg, wu, wd, g, b)


def _two_group_specs(tm, n_a_tiles, n_b_tiles):
    spec_a = pl.BlockSpec((tm, D_MODEL), lambda i: (jnp.minimum(i, n_a_tiles - 1), 0))
    spec_b = pl.BlockSpec((tm, D_MODEL), lambda i: (jnp.clip(i - n_a_tiles, 0, n_b_tiles - 1), 0))
    return spec_a, spec_b


def _route_kernel(xa_ref, xb_ref, wrt_ref, rb_ref, tri_ref, slab_ref, cnt_ref, carry, *, tm, n_a_tiles):
    i = pl.program_id(0)

    @pl.when(i == 0)
    def _():
        carry[...] = jnp.zeros_like(carry)

    x = jnp.where(i < n_a_tiles, xa_ref[...], xb_ref[...])
    lg = _dot_nt(wrt_ref[...], x.astype(BF16)) + rb_ref[...]
    e_iota = lax.broadcasted_iota(jnp.int32, (N_EXPERTS, tm), 0)
    m1 = jnp.max(lg, axis=0, keepdims=True)
    i1 = jnp.min(jnp.where(lg == m1, e_iota, N_EXPERTS), axis=0, keepdims=True)
    lg2 = jnp.where(e_iota == i1, NEG_INF, lg)
    m2 = jnp.max(lg2, axis=0, keepdims=True)
    i2 = jnp.min(jnp.where(lg2 == m2, e_iota, N_EXPERTS), axis=0, keepdims=True)
    e2 = jnp.exp(m2 - m1)
    den = 1.0 + e2
    w1 = 1.0 / den
    w2 = e2 / den
    sel1 = e_iota == i1
    sel2 = e_iota == i2
    oh = jnp.where(sel1 | sel2, 1.0, 0.0)
    pre = _dot(oh.astype(BF16), tri_ref[...]) + carry[:, 0:1]
    r1 = jnp.sum(jnp.where(sel1, pre, 0.0), axis=0, keepdims=True)
    r2 = jnp.sum(jnp.where(sel2, pre, 0.0), axis=0, keepdims=True)
    carry[...] = carry[...] + jnp.sum(oh, axis=1, keepdims=True)
    rows = [i1.astype(F32), i2.astype(F32), r1, r2, w1, w2]
    slab = jnp.zeros((N_EXPERTS, tm), F32)
    for ri, rv in enumerate(rows):
        slab = jnp.where(e_iota == ri, rv, slab)
    slab_ref[...] = slab
    cnt_ref[...] = carry[...]


def _route(xa, xb, wrt, rb, *, tm):
    na, nb_ = xa.shape[0] // tm, xb.shape[0] // tm
    tri = jnp.triu(jnp.ones((tm, tm), BF16), k=1)
    spec_a, spec_b = _two_group_specs(tm, na, nb_)
    return pl.pallas_call(
        functools.partial(_route_kernel, tm=tm, n_a_tiles=na),
        grid=(na + nb_,),
        in_specs=[spec_a, spec_b,
                  _const_spec((N_EXPERTS, D_MODEL)), _const_spec((N_EXPERTS, 1)), _const_spec((tm, tm))],
        out_specs=(pl.BlockSpec((N_EXPERTS, tm), lambda i: (0, i)),
                   pl.BlockSpec((N_EXPERTS, LANES), lambda i: (0, 0))),
        out_shape=(jax.ShapeDtypeStruct((N_EXPERTS, (na + nb_) * tm), F32),
                   jax.ShapeDtypeStruct((N_EXPERTS, LANES), F32)),
        scratch_shapes=[pltpu.VMEM((N_EXPERTS, LANES), F32)],
        compiler_params=pltpu.CompilerParams(dimension_semantics=("arbitrary",)),
        name="moe_route",
    )(xa, xb, wrt, rb, tri)


def _row_copy(src, src_row, dst, dst_row, sem):
    return pltpu.make_async_copy(src.at[pl.ds(src_row, 1)], dst.at[pl.ds(dst_row, 1)], sem)


def _dispatch_kernel(slot_ref, xa_ref, xb_ref, z_ref, wg_ref, wu_ref, wd_ref,
                     xs_ref, wgb_ref, wub_ref, wdb_ref, sem, *, tm, n_a_tiles, n_b_tiles, n_cast):
    i = pl.program_id(0)

    @pl.when(i < n_cast)
    def _():
        wgb_ref[...] = wg_ref[...].astype(BF16)
        wub_ref[...] = wu_ref[...].astype(BF16)
        wdb_ref[...] = wd_ref[...].astype(BF16)

    def issue_from(src_ref):
        def issue(r, c):
            _row_copy(src_ref, r, xs_ref, slot_ref[0, 0, r], sem).start(priority=0)
            _row_copy(src_ref, r, xs_ref, slot_ref[0, 0, tm + r], sem).start(priority=1)
            return c

        lax.fori_loop(0, tm, issue, 0, unroll=8)

    @pl.when(i < n_a_tiles)
    def _():
        issue_from(xa_ref)

    @pl.when((i >= n_a_tiles) & (i < n_a_tiles + n_b_tiles))
    def _():
        issue_from(xb_ref)

    @pl.when(i >= n_a_tiles + n_b_tiles)
    def _():
        issue_from(z_ref)

    for _ in range(2):
        pltpu.make_async_copy(z_ref, xs_ref.at[pl.ds(0, tm)], sem).wait()


def _dispatch(xa, xb, slots3, n_rows, wg, wu, wd, *, tm):
    na, nb_ = xa.shape[0] // tm, xb.shape[0] // tm
    n_steps = slots3.shape[0]
    chunks = n_steps // N_EXPERTS
    n_cast = chunks * N_EXPERTS
    assert chunks > 0 and D_MODEL % (16 * chunks) == 0 and D_FF % (16 * chunks) == 0
    spec_a, spec_b = _two_group_specs(tm, na, nb_)
    zsrc = jnp.zeros((tm, D_MODEL), F32)

    def chunk_spec(rows, cols):
        def imap(i):
            c = jnp.minimum(i, n_cast - 1)
            return (c // chunks, c % chunks, 0)
        return pl.BlockSpec((1, rows // chunks, cols), imap)

    up_spec, down_spec = chunk_spec(D_MODEL, D_FF), chunk_spec(D_FF, D_MODEL)
    return pl.pallas_call(
        functools.partial(_dispatch_kernel, tm=tm, n_a_tiles=na, n_b_tiles=nb_, n_cast=n_cast),
        grid=(n_steps,),
        in_specs=[pl.BlockSpec((1, 1, 2 * tm), lambda i: (i, 0, 0), memory_space=pltpu.SMEM),
                  spec_a, spec_b, _const_spec((tm, D_MODEL)), up_spec, up_spec, down_spec],
        out_specs=(pl.BlockSpec(memory_space=pl.ANY), up_spec, up_spec, down_spec),
        out_shape=(jax.ShapeDtypeStruct((n_rows, D_MODEL), F32),
                   jax.ShapeDtypeStruct(wg.shape, BF16), jax.ShapeDtypeStruct(wu.shape, BF16),
                   jax.ShapeDtypeStruct(wd.shape, BF16)),
        scratch_shapes=[pltpu.SemaphoreType.DMA(())],
        compiler_params=pltpu.CompilerParams(dimension_semantics=("arbitrary",), vmem_limit_bytes=VMEM_LIMIT),
        name="moe_dispatch",
    )(slots3, xa, xb, zsrc, wg, wu, wd)


def _expert_kernel(te_ref, nt_ref, xs_ref, wg_ref, wu_ref, wd_ref, o_ref):
    del te_ref
    live = pl.program_id(0) < nt_ref[0]

    @pl.when(live)
    def _():
        o_ref[...] = _swiglu(xs_ref[...].astype(BF16), wg_ref[0], wu_ref[0], wd_ref[0])

    @pl.when(jnp.logical_not(live))
    def _():
        o_ref[...] = jnp.zeros_like(o_ref)


def _experts(xs, tile_expert, n_tiles, wg, wu, wd, *, tm):
    n_rows = xs.shape[0]

    def row_map(i, te, nt):
        return (jnp.minimum(i, nt[0] - 1), 0)

    def w_map(i, te, nt):
        return (te[i], 0, 0)

    return pl.pallas_call(
        _expert_kernel,
        grid_spec=pltpu.PrefetchScalarGridSpec(
            num_scalar_prefetch=2,
            grid=(n_rows // tm,),
            in_specs=[pl.BlockSpec((tm, D_MODEL), row_map),
                      pl.BlockSpec((1, D_MODEL, D_FF), w_map),
                      pl.BlockSpec((1, D_MODEL, D_FF), w_map),
                      pl.BlockSpec((1, D_FF, D_MODEL), w_map)],
            out_specs=pl.BlockSpec((tm, D_MODEL), lambda i, te, nt: (i, 0)),
        ),
        out_shape=jax.ShapeDtypeStruct((n_rows, D_MODEL), F32),
        compiler_params=pltpu.CompilerParams(dimension_semantics=("arbitrary",), vmem_limit_bytes=VMEM_LIMIT),
        name="moe_experts",
    )(tile_expert, n_tiles, xs, wg, wu, wd)


def _combine_kernel(slot_ref, x_ref, slab_ref, ys_ref, g_ref, b_ref, o_ref, buf0, buf1, sem, *, tm, alpha):
    def issue(r, c):
        _row_copy(ys_ref, slot_ref[0, 0, r], buf0, r, sem).start(priority=0)
        _row_copy(ys_ref, slot_ref[0, 0, tm + r], buf1, r, sem).start(priority=1)
        return c

    lax.fori_loop(0, tm, issue, 0, unroll=8)

    pltpu.make_async_copy(ys_ref.at[pl.ds(0, tm)], buf0, sem).wait()
    pltpu.make_async_copy(ys_ref.at[pl.ds(0, tm)], buf1, sem).wait()
    slab_t = jnp.concatenate([slab_ref[...], jnp.zeros((LANES - N_EXPERTS, tm), F32)], axis=0).T
    y = slab_t[:, 4:5] * buf0[...] + slab_t[:, 5:6] * buf1[...]
    o_ref[...] = _layer_norm(alpha * x_ref[...] + y, g_ref[...], b_ref[...])


def _combine(x, slots3, slab, ys, g, b, *, tm, tile0, alpha, name):
    n = x.shape[0]
    return pl.pallas_call(
        functools.partial(_combine_kernel, tm=tm, alpha=alpha),
        grid=(n // tm,),
        in_specs=[pl.BlockSpec((1, 1, 2 * tm), lambda i: (i + tile0, 0, 0), memory_space=pltpu.SMEM),
                  pl.BlockSpec((tm, D_MODEL), lambda i: (i, 0)),
                  pl.BlockSpec((N_EXPERTS, tm), lambda i: (0, i + tile0)),
                  pl.BlockSpec(memory_space=pl.ANY),
                  _const_spec((1, D_MODEL)), _const_spec((1, D_MODEL))],
        out_specs=pl.BlockSpec((tm, D_MODEL), lambda i: (i, 0)),
        out_shape=jax.ShapeDtypeStruct((n, D_MODEL), F32),
        scratch_shapes=[pltpu.VMEM((tm, D_MODEL), F32), pltpu.VMEM((tm, D_MODEL), F32),
                        pltpu.SemaphoreType.DMA(())],
        compiler_params=pltpu.CompilerParams(dimension_semantics=("arbitrary",)),
        name=name,
    )(slots3, x, slab, ys, g, b)


def _free_slots(counts, offs, padded, n_rows, n_free):
    starts = jnp.concatenate([offs + counts, (offs[-1:] + padded[-1:])])
    gaps = jnp.concatenate([padded - counts, n_rows - (offs[-1:] + padded[-1:])])
    cum = jnp.cumsum(gaps)
    prev = cum - gaps
    idx = jnp.arange(n_free, dtype=jnp.int32)
    seg = jnp.sum(idx[:, None] >= cum[None, :], axis=1)
    pick = seg[:, None] == jnp.arange(N_EXPERTS + 1, dtype=jnp.int32)[None, :]
    return jnp.sum(jnp.where(pick, (starts - prev)[None, :], 0), axis=1) + idx


def _moe(xa, xb, router_w, router_b, wg, wu, wd, g, b, *, alpha, tm_rows, tm_expert):
    na, nb_ = xa.shape[0], xb.shape[0]
    n = na + nb_
    slab, cnt = _route(xa, xb, router_w.T.astype(BF16), router_b.reshape(N_EXPERTS, 1), tm=tm_rows)
    counts = cnt[:, 0].astype(jnp.int32)
    tiles_per = (counts + tm_expert - 1) // tm_expert
    padded = tiles_per * tm_expert
    ends = jnp.cumsum(tiles_per)
    offs = (ends - tiles_per) * tm_expert
    i1 = slab[0].astype(jnp.int32)
    i2 = slab[1].astype(jnp.int32)
    eye = jnp.arange(N_EXPERTS, dtype=jnp.int32)[:, None]
    off1 = jnp.sum(jnp.where(eye == i1[None, :], offs[:, None], 0), axis=0)
    off2 = jnp.sum(jnp.where(eye == i2[None, :], offs[:, None], 0), axis=0)
    slots = jnp.concatenate([(off1 + slab[2].astype(jnp.int32)).reshape(-1, 1, tm_rows),
                             (off2 + slab[3].astype(jnp.int32)).reshape(-1, 1, tm_rows)], axis=2)
    n_free = N_EXPERTS * tm_expert
    n_rows = 2 * n + n_free
    assert n_free % (2 * tm_rows) == 0
    free = _free_slots(counts, offs, padded, n_rows, n_free)
    max_tiles = n_rows // tm_expert
    tile_expert = jnp.minimum(
        jnp.sum(jnp.arange(max_tiles, dtype=jnp.int32)[:, None] >= ends[None, :], axis=1), N_EXPERTS - 1
    ).astype(jnp.int32)
    n_tiles = ends[-1:].astype(jnp.int32)
    slots3 = jnp.concatenate([slots, free.reshape(-1, 1, 2 * tm_rows)], axis=0)

    xs, wg_b, wu_b, wd_b = _dispatch(xa, xb, slots3, n_rows, wg, wu, wd, tm=tm_rows)
    ys = _experts(xs, tile_expert, n_tiles, wg_b, wu_b, wd_b, tm=tm_expert)
    ya = _combine(xa, slots3, slab, ys, g, b, tm=tm_rows, tile0=0, alpha=alpha, name="moe_combine_a")
    yb = _combine(xb, slots3, slab, ys, g, b, tm=tm_rows, tile0=na // tm_rows, alpha=alpha, name="moe_combine_b")
    return ya, yb


def _layer_weights(l, w_in, conv_w, sink, p_attn, p_conv, p_mem, w_o, w_mem_k, w_mem_v, ln1_g, ln1_b):
    return dict(
        sink=sink[l],
        w_in=w_in[l].astype(BF16),
        conv_w=conv_w[l],
        p_attn=p_attn[l].astype(BF16),
        p_conv=p_conv[l].astype(BF16),
        p_mem=p_mem[l].astype(BF16),
        w_o=w_o[l].astype(BF16),
        w_mem_k=w_mem_k[l].astype(BF16),
        w_mem_v=w_mem_v[l].astype(BF16),
        ln1_g=ln1_g[l].reshape(1, D_MODEL),
        ln1_b=ln1_b[l].reshape(1, D_MODEL),
    )


def _feature_major(c):
    d, nb, tok, h, hd = c.shape
    return jnp.transpose(c, (0, 1, 3, 4, 2)).reshape(d, nb, h * hd, tok)


def _token_major(c, heads):
    d, nb, f, tok = c.shape
    return jnp.transpose(c.reshape(d, nb, heads, f // heads, tok), (0, 1, 4, 2, 3))


def kernel(x_prompt, x_sample, mem_prompt, cache_win_k, cache_win_v, cache_conv, cache_mem_k, cache_mem_v, w_in, conv_w, sink, p_attn, p_conv, p_mem, w_o, w_mem_k, w_mem_v, ln1_g, ln1_b, ln2_g, ln2_b, ffn_w_gate, ffn_w_up, ffn_w_down, router_w, router_b, exp_w_gate, exp_w_up, exp_w_down):
    depth = w_in.shape[0]
    alpha = (2 * depth) ** 0.25
    b, seq, _ = x_prompt.shape
    nb, t, _ = x_sample.shape
    n_p = b * seq
    n_s = nb * t

    xp = x_prompt
    xs = x_sample.reshape(n_s, D_MODEL)
    ckt = _feature_major(cache_win_k)
    cvt = _feature_major(cache_win_v)
    cmkt = _feature_major(cache_mem_k)
    cmvt = _feature_major(cache_mem_v)

    outs = {k_: [] for k_ in ("wk_p", "wv_p", "cv_p", "mk_p", "mv_p", "wk_s", "wv_s", "cv_s")}
    for l in range(depth):
        w = _layer_weights(l, w_in, conv_w, sink, p_attn, p_conv, p_mem, w_o, w_mem_k, w_mem_v, ln1_g, ln1_b)
        x1p, wk, wv, cvp, mk, mv = _mixer_prompt(xp, mem_prompt, w, l, ts=1024, alpha=alpha)
        x1s, wks, wvs, cvs = _mixer_sample(xs, ckt, cvt, cache_conv, cmkt, cmvt, w, l, bb=LANES // t, t=t, alpha=alpha)
        outs["wk_p"].append(wk); outs["wv_p"].append(wv); outs["cv_p"].append(cvp)
        outs["mk_p"].append(mk); outs["mv_p"].append(mv)
        outs["wk_s"].append(wks); outs["wv_s"].append(wvs); outs["cv_s"].append(cvs)
        x1p = x1p.reshape(n_p, D_MODEL)
        g2 = ln2_g[l].reshape(1, D_MODEL)
        b2 = ln2_b[l].reshape(1, D_MODEL)
        i = l // 2
        if l % 2 == 0:
            wg, wu, wd = ffn_w_gate[i].astype(BF16), ffn_w_up[i].astype(BF16), ffn_w_down[i].astype(BF16)
            x2p = _ffn_dense(x1p, wg, wu, wd, g2, b2, tm=512, alpha=alpha, name=f"ffn_dense_p_l{l}")
            xs = _ffn_dense(x1s, wg, wu, wd, g2, b2, tm=512, alpha=alpha, name=f"ffn_dense_s_l{l}")
        else:
            x2p, xs = _moe(x1p, x1s, router_w[i], router_b[i], exp_w_gate[i], exp_w_up[i], exp_w_down[i],
                           g2, b2, alpha=alpha, tm_rows=512, tm_expert=256)
        xp = x2p.reshape(b, seq, D_MODEL)

    return (xp, xs.reshape(nb, t, D_MODEL),
            _token_major(jnp.stack(outs["wk_p"]), N_KV_HEADS),
            _token_major(jnp.stack(outs["wv_p"]), N_KV_HEADS),
            jnp.stack(outs["cv_p"]),
            _token_major(jnp.stack(outs["mk_p"]), MEM_HEADS),
            _token_major(jnp.stack(outs["mv_p"]), MEM_HEADS),
            _token_major(jnp.stack(outs["wk_s"]), N_KV_HEADS),
            _token_major(jnp.stack(outs["wv_s"]), N_KV_HEADS),
            jnp.stack(outs["cv_s"]))
```

```python
import functools

import jax
import jax.numpy as jnp
from jax import lax
from jax.experimental import pallas as pl
from jax.experimental.pallas import tpu as pltpu

F32 = jnp.float32
BF16 = jnp.bfloat16

D_MODEL = 1024
HEAD_DIM = 64
N_HEADS = 8
N_KV_HEADS = 2
GROUP = N_HEADS // N_KV_HEADS
ATTN_DIM = N_HEADS * HEAD_DIM
KV_DIM = N_KV_HEADS * HEAD_DIM
WINDOW = 128
ATTN_SCALE = HEAD_DIM ** -0.5
CONV_DIM = 256
CONV_WIDTH = 3
N_MEM = 256
MEM_HEADS = 4
MEM_DIM = MEM_HEADS * HEAD_DIM
D_FF = 2816
N_EXPERTS = 8
LN_EPS = 1e-5

_Q0, _K0, _V0, _H0, _GB0, _GC0, _QM0, _GA0 = 0, 512, 640, 768, 1024, 1280, 1536, 1792
_MIX_COLS = _GA0
IN_DIM = _GA0 + 3 * D_MODEL

LANES = 128
VMEM_LIMIT = 56 * 1024 * 1024

NEG_INF = float("-inf")


def _dup_kv_lanes(kv):
    lo = lax.broadcasted_iota(jnp.int32, kv.shape, 1) < HEAD_DIM
    swapped = pltpu.roll(kv, HEAD_DIM, 1)
    return [jnp.where(lo, kv, swapped).astype(BF16), jnp.where(lo, swapped, kv).astype(BF16)]


def _layer_norm(v, g, b):
    mu = jnp.mean(v, axis=-1, keepdims=True)
    c = v - mu
    var = jnp.mean(c * c, axis=-1, keepdims=True)
    return c * lax.rsqrt(var + LN_EPS) * g + b


def _dot(a, b):
    return jnp.dot(a, b, preferred_element_type=F32)


def _dot_nt(a, b):
    return lax.dot_general(a, b, (((1,), (1,)), ((), ())), preferred_element_type=F32)


def _stack_lane_halves(q, n_blocks):
    m = q.shape[0]
    lo = lax.broadcasted_iota(jnp.int32, (m, LANES), 1) < HEAD_DIM
    parts = []
    for j in range(n_blocks):
        blk = q[:, j * LANES:(j + 1) * LANES]
        parts.append(jnp.where(lo, blk, 0.0))
        parts.append(jnp.where(lo, 0.0, blk))
    return jnp.concatenate(parts, axis=0)


def _unstack_lane_halves(r, m, n_blocks):
    lo = lax.broadcasted_iota(jnp.int32, (m, LANES), 1) < HEAD_DIM
    cols = [jnp.where(lo, r[(2 * j) * m:(2 * j + 1) * m], r[(2 * j + 1) * m:(2 * j + 2) * m])
            for j in range(n_blocks)]
    return jnp.concatenate(cols, axis=1)


def _merge_project_norm(x, zg, o_a, o_c, o_m, p_attn_ref, p_conv_ref, p_mem_ref, w_o_ref, g_ref, b_ref, alpha):
    merged = (jax.nn.sigmoid(zg[:, 0:D_MODEL]) * _dot(o_a.astype(BF16), p_attn_ref[...])
              + jax.nn.sigmoid(zg[:, D_MODEL:2 * D_MODEL]) * _dot(o_c.astype(BF16), p_conv_ref[...])
              + jax.nn.sigmoid(zg[:, 2 * D_MODEL:3 * D_MODEL]) * _dot(o_m.astype(BF16), p_mem_ref[...]))
    mix = _dot(merged.astype(BF16), w_o_ref[...])
    return _layer_norm(alpha * x + mix, g_ref[...], b_ref[...])


def _const_spec(shape):
    n = len(shape)
    return pl.BlockSpec(shape, lambda *_: (0,) * n, pipeline_mode=pl.Buffered(1))


def _mixer_prompt_kernel(sink_ref, x_ref, mem_ref, w_in_ref, convw_ref, p_attn_ref, p_conv_ref, p_mem_ref,
                         w_o_ref, wmk_ref, wmv_ref, g_ref, b_ref,
                         x1_ref, wk_ref, wv_ref, cv_ref, mk_ref, mv_ref,
                         kprev, vprev, uprev, mk_s, mv_s, *, ts, alpha):
    s = pl.program_id(1)
    ns = pl.num_programs(1)
    x = x_ref[0]
    xb = x.astype(BF16)

    @pl.when(s == 0)
    def _():
        memb = mem_ref[0].astype(BF16)
        mk = _dot(memb, wmk_ref[...])
        mv = _dot(memb, wmv_ref[...])
        mk_ref[0] = mk.T
        mv_ref[0] = mv.T
        mk_s[...] = mk.astype(BF16)
        mv_s[...] = mv.astype(BF16)
        kprev[...] = jnp.zeros_like(kprev)
        vprev[...] = jnp.zeros_like(vprev)
        uprev[...] = jnp.zeros_like(uprev)

    z = _dot(xb, w_in_ref[:, 0:_MIX_COLS])
    q = z[:, _Q0:_K0] * ATTN_SCALE
    k = z[:, _K0:_V0]
    v = z[:, _V0:_H0]
    hc = z[:, _H0:_GB0]
    g_b = z[:, _GB0:_GC0]
    g_c = z[:, _GC0:_QM0]
    qm = z[:, _QM0:_GA0] * ATTN_SCALE

    k_ext = jnp.concatenate([kprev[...], k], axis=0)
    v_ext = jnp.concatenate([vprev[...], v], axis=0)
    kd = _dup_kv_lanes(k_ext)
    vd = _dup_kv_lanes(v_ext)
    half = GROUP * WINDOW
    row_i = lax.broadcasted_iota(jnp.int32, (WINDOW, 2 * WINDOW), 0)
    col_j = lax.broadcasted_iota(jnp.int32, (WINDOW, 2 * WINDOW), 1)
    band = (col_j > row_i) & (col_j <= row_i + WINDOW)
    band_first = band & ((col_j >= WINDOW) | (s > 0))
    o_blocks = []
    for blk in range(ts // WINDOW):
        qs = _stack_lane_halves(q[blk * WINDOW:(blk + 1) * WINDOW], GROUP).astype(BF16)
        band_rows = slice(blk * WINDOW, blk * WINDOW + 2 * WINDOW)
        sc = jnp.concatenate([_dot_nt(qs[gi * half:(gi + 1) * half], kd[gi][band_rows])
                              for gi in range(N_KV_HEADS)], axis=0)
        mask = band_first if blk == 0 else band
        ps = []
        for hh in range(N_HEADS):
            sink_h = sink_ref[hh]
            sch = jnp.where(mask, sc[hh * WINDOW:(hh + 1) * WINDOW], NEG_INF)
            m = jnp.maximum(jnp.max(sch, axis=-1, keepdims=True), sink_h)
            e = jnp.exp(sch - m)
            den = jnp.sum(e, axis=-1, keepdims=True) + jnp.exp(sink_h - m)
            ps.append((e / den).astype(BF16))
        r = jnp.concatenate([_dot(jnp.concatenate(ps[gi * GROUP:(gi + 1) * GROUP], axis=0), vd[gi][band_rows])
                             for gi in range(N_KV_HEADS)], axis=0)
        o_blocks.append(_unstack_lane_halves(r, WINDOW, GROUP))
    o_a = jnp.concatenate(o_blocks, axis=0)
    kprev[...] = k_ext[ts:ts + WINDOW]
    vprev[...] = v_ext[ts:ts + WINDOW]

    u = g_c * hc
    rows = lax.broadcasted_iota(jnp.int32, (ts, CONV_DIM), 0)
    up1 = uprev[7:8, :]
    up2 = uprev[6:7, :]
    u_m1 = jnp.where(rows == 0, up1, pltpu.roll(u, 1, 0))
    u_m2 = jnp.where(rows == 0, up2, jnp.where(rows == 1, up1, pltpu.roll(u, 2, 0)))
    cw = convw_ref[...]
    conv = cw[0:1, :] * u_m2 + cw[1:2, :] * u_m1 + cw[2:3, :] * u
    o_c = g_b * conv
    uprev[...] = u[ts - 8:ts]

    qms = _stack_lane_halves(qm, MEM_HEADS // 2).astype(BF16)
    o_cols = []
    lo = lax.broadcasted_iota(jnp.int32, (ts, LANES), 1) < HEAD_DIM
    for jb in range(MEM_HEADS // 2):
        scm = _dot_nt(qms[2 * jb * ts:(2 * jb + 2) * ts], mk_s[:, jb * LANES:(jb + 1) * LANES])
        mm = jnp.max(scm, axis=-1, keepdims=True)
        em = jnp.exp(scm - mm)
        pm = (em / jnp.sum(em, axis=-1, keepdims=True)).astype(BF16)
        rm = _dot(pm, mv_s[:, jb * LANES:(jb + 1) * LANES])
        o_cols.append(jnp.where(lo, rm[0:ts], rm[ts:2 * ts]))
    o_m = jnp.concatenate(o_cols, axis=1)

    zg = _dot(xb, w_in_ref[:, _GA0:IN_DIM])
    x1_ref[0] = _merge_project_norm(x, zg, o_a, o_c, o_m, p_attn_ref, p_conv_ref, p_mem_ref, w_o_ref,
                                    g_ref, b_ref, alpha)

    @pl.when(s == ns - 1)
    def _():
        wk_ref[0] = k[ts - WINDOW:ts].T
        wv_ref[0] = v[ts - WINDOW:ts].T
        cv_ref[0] = u[ts - (CONV_WIDTH - 1):ts]


def _mixer_prompt(x, mem, w, l, *, ts, alpha):
    b, seq, _ = x.shape
    kern = functools.partial(_mixer_prompt_kernel, ts=ts, alpha=alpha)
    smem = pl.BlockSpec(memory_space=pltpu.SMEM)
    in_specs = [
        smem,
        pl.BlockSpec((1, ts, D_MODEL), lambda i, s: (i, s, 0)),
        pl.BlockSpec((1, N_MEM, D_MODEL), lambda i, s: (i, 0, 0)),
        _const_spec((D_MODEL, IN_DIM)),
        _const_spec((CONV_WIDTH, CONV_DIM)),
        _const_spec((ATTN_DIM, D_MODEL)),
        _const_spec((CONV_DIM, D_MODEL)),
        _const_spec((MEM_DIM, D_MODEL)),
        _const_spec((D_MODEL, D_MODEL)),
        _const_spec((D_MODEL, MEM_DIM)),
        _const_spec((D_MODEL, MEM_DIM)),
        _const_spec((1, D_MODEL)),
        _const_spec((1, D_MODEL)),
    ]
    out_shape = (
        jax.ShapeDtypeStruct((b, seq, D_MODEL), F32),
        jax.ShapeDtypeStruct((b, KV_DIM, WINDOW), F32),
        jax.ShapeDtypeStruct((b, KV_DIM, WINDOW), F32),
        jax.ShapeDtypeStruct((b, CONV_WIDTH - 1, CONV_DIM), F32),
        jax.ShapeDtypeStruct((b, MEM_DIM, N_MEM), F32),
        jax.ShapeDtypeStruct((b, MEM_DIM, N_MEM), F32),
    )
    out_specs = (
        pl.BlockSpec((1, ts, D_MODEL), lambda i, s: (i, s, 0)),
        pl.BlockSpec((1, KV_DIM, WINDOW), lambda i, s: (i, 0, 0)),
        pl.BlockSpec((1, KV_DIM, WINDOW), lambda i, s: (i, 0, 0)),
        pl.BlockSpec((1, CONV_WIDTH - 1, CONV_DIM), lambda i, s: (i, 0, 0)),
        pl.BlockSpec((1, MEM_DIM, N_MEM), lambda i, s: (i, 0, 0)),
        pl.BlockSpec((1, MEM_DIM, N_MEM), lambda i, s: (i, 0, 0)),
    )
    scratch = [
        pltpu.VMEM((WINDOW, KV_DIM), F32),
        pltpu.VMEM((WINDOW, KV_DIM), F32),
        pltpu.VMEM((8, CONV_DIM), F32),
        pltpu.VMEM((N_MEM, MEM_DIM), BF16),
        pltpu.VMEM((N_MEM, MEM_DIM), BF16),
    ]
    return pl.pallas_call(
        kern,
        grid=(b, seq // ts),
        in_specs=in_specs,
        out_specs=out_specs,
        out_shape=out_shape,
        scratch_shapes=scratch,
        compiler_params=pltpu.CompilerParams(dimension_semantics=("arbitrary", "arbitrary"),
                                             vmem_limit_bytes=VMEM_LIMIT),
        name=f"mixer_prompt_l{l}",
    )(w["sink"], x, mem, w["w_in"], w["conv_w"], w["p_attn"], w["p_conv"], w["p_mem"], w["w_o"],
      w["w_mem_k"], w["w_mem_v"], w["ln1_g"], w["ln1_b"])


def _mixer_sample_kernel(sink_ref, x_ref, ck_ref, cvv_ref, cc_ref, cmk_ref, cmv_ref, w_in_ref, convw_ref,
                         p_attn_ref, p_conv_ref, p_mem_ref, w_o_ref, g_ref, b_ref,
                         x1_ref, wk_ref, wv_ref, cv_ref, *, bb, t, alpha):
    nbuf = ck_ref.shape[2]
    x = x_ref[...]
    xb = x.astype(BF16)
    z = _dot(xb, w_in_ref[:, 0:_MIX_COLS])
    q = z[:, _Q0:_K0] * ATTN_SCALE
    u = z[:, _GC0:_QM0] * z[:, _H0:_GB0]
    g_b = z[:, _GB0:_GC0]
    qm = z[:, _QM0:_GA0] * ATTN_SCALE
    cw = convw_ref[...]
    kt_new = z[:, _K0:_V0].T
    vt_new = z[:, _V0:_H0].T

    def dup_rows(kv_t):
        kvb = kv_t.astype(BF16)
        return [jnp.concatenate([kvb[gi * HEAD_DIM:(gi + 1) * HEAD_DIM]] * 2, axis=0) for gi in range(N_KV_HEADS)]

    nstk = N_HEADS * t
    r_t = lax.broadcasted_iota(jnp.int32, (nstk, 2 * WINDOW), 0) % t
    c_j = lax.broadcasted_iota(jnp.int32, (nstk, 2 * WINDOW), 1)
    dist = r_t + nbuf - c_j
    wmask = (dist >= 0) & (dist < WINDOW) & (c_j < nbuf + t)
    blk_id = lax.broadcasted_iota(jnp.int32, (nstk, 1), 0) // t
    sink_col = jnp.zeros((nstk, 1), F32)
    for hh in range(N_HEADS):
        sink_col = jnp.where(blk_id == hh, sink_ref[hh], sink_col)
    mlane = lax.broadcasted_iota(jnp.int32, (t, MEM_DIM), 1) // HEAD_DIM
    klane = lax.broadcasted_iota(jnp.int32, (KV_DIM, LANES), 1)

    o_a_rows, o_c_rows, o_m_rows = [], [], []
    for bi in range(bb):
        sl = slice(bi * t, (bi + 1) * t)
        kc = ck_ref[bi]
        vc = cvv_ref[bi]
        k_new0 = jnp.where(klane < t, pltpu.roll(kt_new, (LANES - bi * t) % LANES, 1), 0.0)
        v_new0 = jnp.where(klane < t, pltpu.roll(vt_new, (LANES - bi * t) % LANES, 1), 0.0)
        kd = dup_rows(jnp.concatenate([kc, k_new0], axis=1))
        vd = dup_rows(jnp.concatenate([vc, v_new0], axis=1))
        qs = _stack_lane_halves(q[sl], GROUP).astype(BF16)
        hrows = GROUP * t
        sc = jnp.concatenate([_dot(qs[gi * hrows:(gi + 1) * hrows], kd[gi]) for gi in range(N_KV_HEADS)], axis=0)
        sc = jnp.where(wmask, sc, NEG_INF)
        m = jnp.maximum(jnp.max(sc, axis=-1, keepdims=True), sink_col)
        e = jnp.exp(sc - m)
        den = jnp.sum(e, axis=-1, keepdims=True) + jnp.exp(sink_col - m)
        p = (e / den).astype(BF16)
        r = jnp.concatenate([_dot_nt(p[gi * hrows:(gi + 1) * hrows], vd[gi]) for gi in range(N_KV_HEADS)], axis=0)
        o_a_rows.append(_unstack_lane_halves(r, t, GROUP))
        tail = (nbuf - t - bi * t) % LANES
        wk_ref[bi] = jnp.where(klane < nbuf - t, pltpu.roll(kc, nbuf - t, 1), pltpu.roll(kt_new, tail, 1))
        wv_ref[bi] = jnp.where(klane < nbuf - t, pltpu.roll(vc, nbuf - t, 1), pltpu.roll(vt_new, tail, 1))
        u_ext = jnp.concatenate([cc_ref[bi], u[sl]], axis=0)
        conv = cw[0:1, :] * u_ext[0:t] + cw[1:2, :] * u_ext[1:t + 1] + cw[2:3, :] * u_ext[2:t + 2]
        o_c_rows.append(g_b[sl] * conv)
        cv_ref[bi] = u_ext[t:t + CONV_WIDTH - 1]
        qmb = qm[sl]
        qms = jnp.concatenate([jnp.where(mlane == hh, qmb, 0.0) for hh in range(MEM_HEADS)], axis=0).astype(BF16)
        scm = _dot(qms, cmk_ref[bi].astype(BF16))
        mm = jnp.max(scm, axis=-1, keepdims=True)
        em = jnp.exp(scm - mm)
        pm = (em / jnp.sum(em, axis=-1, keepdims=True)).astype(BF16)
        rm = _dot_nt(pm, cmv_ref[bi].astype(BF16))
        om = jnp.zeros((t, MEM_DIM), F32)
        for hh in range(MEM_HEADS):
            om = jnp.where(mlane == hh, rm[hh * t:(hh + 1) * t], om)
        o_m_rows.append(om)

    o_a = jnp.concatenate(o_a_rows, axis=0)
    o_c = jnp.concatenate(o_c_rows, axis=0)
    o_m = jnp.concatenate(o_m_rows, axis=0)
    zg = _dot(xb, w_in_ref[:, _GA0:IN_DIM])
    x1_ref[...] = _merge_project_norm(x, zg, o_a, o_c, o_m, p_attn_ref, p_conv_ref, p_mem_ref, w_o_ref,
                                      g_ref, b_ref, alpha)


def _mixer_sample(x2d, ckt, cvt, cc, cmkt, cmvt, w, l, *, bb, t, alpha):
    nb, nbuf = ckt.shape[1], ckt.shape[3]
    assert nbuf == LANES and bb * t == LANES and t % 8 == 0
    kern = functools.partial(_mixer_sample_kernel, bb=bb, t=t, alpha=alpha)
    smem = pl.BlockSpec(memory_space=pltpu.SMEM)

    def cache_spec(d1, d2):
        return pl.BlockSpec((None, bb, d1, d2), lambda i: (l, i, 0, 0))

    in_specs = [
        smem,
        pl.BlockSpec((bb * t, D_MODEL), lambda i: (i, 0)),
        cache_spec(KV_DIM, nbuf),
        cache_spec(KV_DIM, nbuf),
        cache_spec(CONV_WIDTH - 1, CONV_DIM),
        cache_spec(MEM_DIM, N_MEM),
        cache_spec(MEM_DIM, N_MEM),
        _const_spec((D_MODEL, IN_DIM)),
        _const_spec((CONV_WIDTH, CONV_DIM)),
        _const_spec((ATTN_DIM, D_MODEL)),
        _const_spec((CONV_DIM, D_MODEL)),
        _const_spec((MEM_DIM, D_MODEL)),
        _const_spec((D_MODEL, D_MODEL)),
        _const_spec((1, D_MODEL)),
        _const_spec((1, D_MODEL)),
    ]
    out_shape = (
        jax.ShapeDtypeStruct((nb * t, D_MODEL), F32),
        jax.ShapeDtypeStruct((nb, KV_DIM, nbuf), F32),
        jax.ShapeDtypeStruct((nb, KV_DIM, nbuf), F32),
        jax.ShapeDtypeStruct((nb, CONV_WIDTH - 1, CONV_DIM), F32),
    )
    out_specs = (
        pl.BlockSpec((bb * t, D_MODEL), lambda i: (i, 0)),
        pl.BlockSpec((bb, KV_DIM, nbuf), lambda i: (i, 0, 0)),
        pl.BlockSpec((bb, KV_DIM, nbuf), lambda i: (i, 0, 0)),
        pl.BlockSpec((bb, CONV_WIDTH - 1, CONV_DIM), lambda i: (i, 0, 0)),
    )
    return pl.pallas_call(
        kern,
        grid=(nb // bb,),
        in_specs=in_specs,
        out_specs=out_specs,
        out_shape=out_shape,
        compiler_params=pltpu.CompilerParams(dimension_semantics=("arbitrary",),
                                             vmem_limit_bytes=VMEM_LIMIT),
        name=f"mixer_sample_l{l}",
    )(w["sink"], x2d, ckt, cvt, cc, cmkt, cmvt, w["w_in"], w["conv_w"], w["p_attn"], w["p_conv"],
      w["p_mem"], w["w_o"], w["ln1_g"], w["ln1_b"])


def _swiglu(xb, wg, wu, wd):
    g = _dot(xb, wg)
    u = _dot(xb, wu)
    return _dot((jax.nn.silu(g) * u).astype(BF16), wd)


def _ffn_kernel(x_ref, wg_ref, wu_ref, wd_ref, g_ref, b_ref, o_ref, *, alpha):
    x = x_ref[...]
    y = _swiglu(x.astype(BF16), wg_ref[...], wu_ref[...], wd_ref[...])
    o_ref[...] = _layer_norm(alpha * x + y, g_ref[...], b_ref[...])


def _ffn_dense(x, wg, wu, wd, g, b, *, tm, alpha, name):
    n = x.shape[0]
    return pl.pallas_call(
        functools.partial(_ffn_kernel, alpha=alpha),
        grid=(n // tm,),
        in_specs=[pl.BlockSpec((tm, D_MODEL), lambda i: (i, 0)),
                  _const_spec((D_MODEL, D_FF)), _const_spec((D_MODEL, D_FF)), _const_spec((D_FF, D_MODEL)),
                  _const_spec((1, D_MODEL)), _const_spec((1, D_MODEL))],
        out_specs=pl.BlockSpec((tm, D_MODEL), lambda i: (i, 0)),
        out_shape=jax.ShapeDtypeStruct((n, D_MODEL), F32),
        compiler_params=pltpu.CompilerParams(dimension_semantics=("arbitrary",), vmem_limit_bytes=VMEM_LIMIT),
        name=name,
    )(x, wg, wu, wd, g, b)


def _two_group_specs(tm, n_a_tiles, n_b_tiles):
    spec_a = pl.BlockSpec((tm, D_MODEL), lambda i: (jnp.minimum(i, n_a_tiles - 1), 0))
    spec_b = pl.BlockSpec((tm, D_MODEL), lambda i: (jnp.clip(i - n_a_tiles, 0, n_b_tiles - 1), 0))
    return spec_a, spec_b


def _route_kernel(xa_ref, xb_ref, wrt_ref, rb_ref, tri_ref, slab_ref, cnt_ref, carry, *, tm, n_a_tiles):
    i = pl.program_id(0)

    @pl.when(i == 0)
    def _():
        carry[...] = jnp.zeros_like(carry)

    x = jnp.where(i < n_a_tiles, xa_ref[...], xb_ref[...])
    lg = _dot_nt(wrt_ref[...], x.astype(BF16)) + rb_ref[...]
    e_iota = lax.broadcasted_iota(jnp.int32, (N_EXPERTS, tm), 0)
    m1 = jnp.max(lg, axis=0, keepdims=True)
    i1 = jnp.min(jnp.where(lg == m1, e_iota, N_EXPERTS), axis=0, keepdims=True)
    lg2 = jnp.where(e_iota == i1, NEG_INF, lg)
    m2 = jnp.max(lg2, axis=0, keepdims=True)
    i2 = jnp.min(jnp.where(lg2 == m2, e_iota, N_EXPERTS), axis=0, keepdims=True)
    e2 = jnp.exp(m2 - m1)
    den = 1.0 + e2
    w1 = 1.0 / den
    w2 = e2 / den
    sel1 = e_iota == i1
    sel2 = e_iota == i2
    oh = jnp.where(sel1 | sel2, 1.0, 0.0)
    pre = _dot(oh.astype(BF16), tri_ref[...]) + carry[:, 0:1]
    r1 = jnp.sum(jnp.where(sel1, pre, 0.0), axis=0, keepdims=True)
    r2 = jnp.sum(jnp.where(sel2, pre, 0.0), axis=0, keepdims=True)
    carry[...] = carry[...] + jnp.sum(oh, axis=1, keepdims=True)
    rows = [i1.astype(F32), i2.astype(F32), r1, r2, w1, w2]
    slab = jnp.zeros((N_EXPERTS, tm), F32)
    for ri, rv in enumerate(rows):
        slab = jnp.where(e_iota == ri, rv, slab)
    slab_ref[...] = slab
    cnt_ref[...] = carry[...]


def _route(xa, xb, wrt, rb, *, tm):
    na, nb_ = xa.shape[0] // tm, xb.shape[0] // tm
    tri = jnp.triu(jnp.ones((tm, tm), BF16), k=1)
    spec_a, spec_b = _two_group_specs(tm, na, nb_)
    return pl.pallas_call(
        functools.partial(_route_kernel, tm=tm, n_a_tiles=na),
        grid=(na + nb_,),
        in_specs=[spec_a, spec_b,
                  _const_spec((N_EXPERTS, D_MODEL)), _const_spec((N_EXPERTS, 1)), _const_spec((tm, tm))],
        out_specs=(pl.BlockSpec((N_EXPERTS, tm), lambda i: (0, i)),
                   pl.BlockSpec((N_EXPERTS, LANES), lambda i: (0, 0))),
        out_shape=(jax.ShapeDtypeStruct((N_EXPERTS, (na + nb_) * tm), F32),
                   jax.ShapeDtypeStruct((N_EXPERTS, LANES), F32)),
        scratch_shapes=[pltpu.VMEM((N_EXPERTS, LANES), F32)],
        compiler_params=pltpu.CompilerParams(dimension_semantics=("arbitrary",)),
        name="moe_route",
    )(xa, xb, wrt, rb, tri)


def _row_copy(src, src_row, dst, dst_row, sem):
    return pltpu.make_async_copy(src.at[pl.ds(src_row, 1)], dst.at[pl.ds(dst_row, 1)], sem)


def _dispatch_kernel(slot_ref, xa_ref, xb_ref, z_ref, wg_ref, wu_ref, wd_ref,
                     xs_ref, wgb_ref, wub_ref, wdb_ref, sem, *, tm, n_a_tiles, n_b_tiles, n_cast):
    i = pl.program_id(0)

    @pl.when(i < n_cast)
    def _():
        wgb_ref[...] = wg_ref[...].astype(BF16)
        wub_ref[...] = wu_ref[...].astype(BF16)
        wdb_ref[...] = wd_ref[...].astype(BF16)

    def issue_from(src_ref):
        def issue(r, c):
            _row_copy(src_ref, r, xs_ref, slot_ref[0, 0, r], sem).start(priority=0)
            _row_copy(src_ref, r, xs_ref, slot_ref[0, 0, tm + r], sem).start(priority=1)
            return c

        lax.fori_loop(0, tm, issue, 0, unroll=16)

    @pl.when(i < n_a_tiles)
    def _():
        issue_from(xa_ref)

    @pl.when((i >= n_a_tiles) & (i < n_a_tiles + n_b_tiles))
    def _():
        issue_from(xb_ref)

    @pl.when(i >= n_a_tiles + n_b_tiles)
    def _():
        issue_from(z_ref)

    for _ in range(2):
        pltpu.make_async_copy(z_ref, xs_ref.at[pl.ds(0, tm)], sem).wait()


def _dispatch(xa, xb, slots3, n_rows, wg, wu, wd, *, tm):
    na, nb_ = xa.shape[0] // tm, xb.shape[0] // tm
    n_steps = slots3.shape[0]
    chunks = n_steps // N_EXPERTS
    n_cast = chunks * N_EXPERTS
    assert chunks > 0 and D_MODEL % (16 * chunks) == 0 and D_FF % (16 * chunks) == 0
    spec_a, spec_b = _two_group_specs(tm, na, nb_)
    zsrc = jnp.zeros((tm, D_MODEL), F32)

    def chunk_spec(rows, cols):
        def imap(i):
            c = jnp.minimum(i, n_cast - 1)
            return (c // chunks, c % chunks, 0)
        return pl.BlockSpec((1, rows // chunks, cols), imap)

    up_spec, down_spec = chunk_spec(D_MODEL, D_FF), chunk_spec(D_FF, D_MODEL)
    return pl.pallas_call(
        functools.partial(_dispatch_kernel, tm=tm, n_a_tiles=na, n_b_tiles=nb_, n_cast=n_cast),
        grid=(n_steps,),
        in_specs=[pl.BlockSpec((1, 1, 2 * tm), lambda i: (i, 0, 0), memory_space=pltpu.SMEM),
                  spec_a, spec_b, _const_spec((tm, D_MODEL)), up_spec, up_spec, down_spec],
        out_specs=(pl.BlockSpec(memory_space=pl.ANY), up_spec, up_spec, down_spec),
        out_shape=(jax.ShapeDtypeStruct((n_rows, D_MODEL), F32),
                   jax.ShapeDtypeStruct(wg.shape, BF16), jax.ShapeDtypeStruct(wu.shape, BF16),
                   jax.ShapeDtypeStruct(wd.shape, BF16)),
        scratch_shapes=[pltpu.SemaphoreType.DMA(())],
        compiler_params=pltpu.CompilerParams(dimension_semantics=("arbitrary",), vmem_limit_bytes=VMEM_LIMIT),
        name="moe_dispatch",
    )(slots3, xa, xb, zsrc, wg, wu, wd)


def _expert_kernel(te_ref, nt_ref, xs_ref, wg_ref, wu_ref, wd_ref, o_ref):
    del te_ref
    live = pl.program_id(0) < nt_ref[0]

    @pl.when(live)
    def _():
        o_ref[...] = _swiglu(xs_ref[...].astype(BF16), wg_ref[0], wu_ref[0], wd_ref[0])

    @pl.when(jnp.logical_not(live))
    def _():
        o_ref[...] = jnp.zeros_like(o_ref)


def _experts(xs, tile_expert, n_tiles, wg, wu, wd, *, tm):
    n_rows = xs.shape[0]

    def row_map(i, te, nt):
        return (jnp.minimum(i, nt[0] - 1), 0)

    def w_map(i, te, nt):
        return (te[i], 0, 0)

    return pl.pallas_call(
        _expert_kernel,
        grid_spec=pltpu.PrefetchScalarGridSpec(
            num_scalar_prefetch=2,
            grid=(n_rows // tm,),
            in_specs=[pl.BlockSpec((tm, D_MODEL), row_map),
                      pl.BlockSpec((1, D_MODEL, D_FF), w_map),
                      pl.BlockSpec((1, D_MODEL, D_FF), w_map),
                      pl.BlockSpec((1, D_FF, D_MODEL), w_map)],
            out_specs=pl.BlockSpec((tm, D_MODEL), lambda i, te, nt: (i, 0)),
        ),
        out_shape=jax.ShapeDtypeStruct((n_rows, D_MODEL), F32),
        compiler_params=pltpu.CompilerParams(dimension_semantics=("arbitrary",), vmem_limit_bytes=VMEM_LIMIT),
        name="moe_experts",
    )(tile_expert, n_tiles, xs, wg, wu, wd)


def _combine_kernel(slot_ref, x_ref, slab_ref, ys_ref, g_ref, b_ref, o_ref, buf0, buf1, sem, *, tm, alpha):
    def issue(r, c):
        _row_copy(ys_ref, slot_ref[0, 0, r], buf0, r, sem).start(priority=0)
        _row_copy(ys_ref, slot_ref[0, 0, tm + r], buf1, r, sem).start(priority=1)
        return c

    lax.fori_loop(0, tm, issue, 0, unroll=16)

    pltpu.make_async_copy(ys_ref.at[pl.ds(0, tm)], buf0, sem).wait()
    pltpu.make_async_copy(ys_ref.at[pl.ds(0, tm)], buf1, sem).wait()
    slab_t = jnp.concatenate([slab_ref[...], jnp.zeros((LANES - N_EXPERTS, tm), F32)], axis=0).T
    y = slab_t[:, 4:5] * buf0[...] + slab_t[:, 5:6] * buf1[...]
    o_ref[...] = _layer_norm(alpha * x_ref[...] + y, g_ref[...], b_ref[...])


def _combine(x, slots3, slab, ys, g, b, *, tm, tile0, alpha, name):
    n = x.shape[0]
    return pl.pallas_call(
        functools.partial(_combine_kernel, tm=tm, alpha=alpha),
        grid=(n // tm,),
        in_specs=[pl.BlockSpec((1, 1, 2 * tm), lambda i: (i + tile0, 0, 0), memory_space=pltpu.SMEM),
                  pl.BlockSpec((tm, D_MODEL), lambda i: (i, 0)),
                  pl.BlockSpec((N_EXPERTS, tm), lambda i: (0, i + tile0)),
                  pl.BlockSpec(memory_space=pl.ANY),
                  _const_spec((1, D_MODEL)), _const_spec((1, D_MODEL))],
        out_specs=pl.BlockSpec((tm, D_MODEL), lambda i: (i, 0)),
        out_shape=jax.ShapeDtypeStruct((n, D_MODEL), F32),
        scratch_shapes=[pltpu.VMEM((tm, D_MODEL), F32), pltpu.VMEM((tm, D_MODEL), F32),
                        pltpu.SemaphoreType.DMA(())],
        compiler_params=pltpu.CompilerParams(dimension_semantics=("arbitrary",)),
        name=name,
    )(slots3, x, slab, ys, g, b)


def _free_slots(counts, offs, padded, n_rows, n_free):
    starts = jnp.concatenate([offs + counts, (offs[-1:] + padded[-1:])])
    gaps = jnp.concatenate([padded - counts, n_rows - (offs[-1:] + padded[-1:])])
    cum = jnp.cumsum(gaps)
    prev = cum - gaps
    idx = jnp.arange(n_free, dtype=jnp.int32)
    seg = jnp.sum(idx[:, None] >= cum[None, :], axis=1)
    pick = seg[:, None] == jnp.arange(N_EXPERTS + 1, dtype=jnp.int32)[None, :]
    return jnp.sum(jnp.where(pick, (starts - prev)[None, :], 0), axis=1) + idx


def _moe(xa, xb, router_w, router_b, wg, wu, wd, g, b, *, alpha, tm_rows, tm_expert):
    na, nb_ = xa.shape[0], xb.shape[0]
    n = na + nb_
    slab, cnt = _route(xa, xb, router_w.T.astype(BF16), router_b.reshape(N_EXPERTS, 1), tm=tm_rows)
    counts = cnt[:, 0].astype(jnp.int32)
    tiles_per = (counts + tm_expert - 1) // tm_expert
    padded = tiles_per * tm_expert
    ends = jnp.cumsum(tiles_per)
    offs = (ends - tiles_per) * tm_expert
    i1 = slab[0].astype(jnp.int32)
    i2 = slab[1].astype(jnp.int32)
    eye = jnp.arange(N_EXPERTS, dtype=jnp.int32)[:, None]
    off1 = jnp.sum(jnp.where(eye == i1[None, :], offs[:, None], 0), axis=0)
    off2 = jnp.sum(jnp.where(eye == i2[None, :], offs[:, None], 0), axis=0)
    slots = jnp.concatenate([(off1 + slab[2].astype(jnp.int32)).reshape(-1, 1, tm_rows),
                             (off2 + slab[3].astype(jnp.int32)).reshape(-1, 1, tm_rows)], axis=2)
    n_free = N_EXPERTS * tm_expert
    n_rows = 2 * n + n_free
    assert n_free % (2 * tm_rows) == 0
    free = _free_slots(counts, offs, padded, n_rows, n_free)
    max_tiles = n_rows // tm_expert
    tile_expert = jnp.minimum(
        jnp.sum(jnp.arange(max_tiles, dtype=jnp.int32)[:, None] >= ends[None, :], axis=1), N_EXPERTS - 1
    ).astype(jnp.int32)
    n_tiles = ends[-1:].astype(jnp.int32)
    slots3 = jnp.concatenate([slots, free.reshape(-1, 1, 2 * tm_rows)], axis=0)

    xs, wg_b, wu_b, wd_b = _dispatch(xa, xb, slots3, n_rows, wg, wu, wd, tm=tm_rows)
    ys = _experts(xs, tile_expert, n_tiles, wg_b, wu_b, wd_b, tm=tm_expert)
    ya = _combine(xa, slots3, slab, ys, g, b, tm=tm_rows, tile0=0, alpha=alpha, name="moe_combine_a")
    yb = _combine(xb, slots3, slab, ys, g, b, tm=tm_rows, tile0=na // tm_rows, alpha=alpha, name="moe_combine_b")
    return ya, yb


def _layer_weights(l, w_in, conv_w, sink, p_attn, p_conv, p_mem, w_o, w_mem_k, w_mem_v, ln1_g, ln1_b):
    return dict(
        sink=sink[l],
        w_in=w_in[l].astype(BF16),
        conv_w=conv_w[l],
        p_attn=p_attn[l].astype(BF16),
        p_conv=p_conv[l].astype(BF16),
        p_mem=p_mem[l].astype(BF16),
        w_o=w_o[l].astype(BF16),
        w_mem_k=w_mem_k[l].astype(BF16),
        w_mem_v=w_mem_v[l].astype(BF16),
        ln1_g=ln1_g[l].reshape(1, D_MODEL),
        ln1_b=ln1_b[l].reshape(1, D_MODEL),
    )


def _feature_major(c):
    d, nb, tok, h, hd = c.shape
    return jnp.transpose(c, (0, 1, 3, 4, 2)).reshape(d, nb, h * hd, tok)


def _token_major(c, heads):
    d, nb, f, tok = c.shape
    return jnp.transpose(c.reshape(d, nb, heads, f // heads, tok), (0, 1, 4, 2, 3))


def kernel(x_prompt, x_sample, mem_prompt, cache_win_k, cache_win_v, cache_conv, cache_mem_k, cache_mem_v, w_in, conv_w, sink, p_attn, p_conv, p_mem, w_o, w_mem_k, w_mem_v, ln1_g, ln1_b, ln2_g, ln2_b, ffn_w_gate, ffn_w_up, ffn_w_down, router_w, router_b, exp_w_gate, exp_w_up, exp_w_down):
    depth = w_in.shape[0]
    alpha = (2 * depth) ** 0.25
    b, seq, _ = x_prompt.shape
    nb, t, _ = x_sample.shape
    n_p = b * seq
    n_s = nb * t

    xp = x_prompt
    xs = x_sample.reshape(n_s, D_MODEL)
    ckt = _feature_major(cache_win_k)
    cvt = _feature_major(cache_win_v)
    cmkt = _feature_major(cache_mem_k)
    cmvt = _feature_major(cache_mem_v)

    outs = {k_: [] for k_ in ("wk_p", "wv_p", "cv_p", "mk_p", "mv_p", "wk_s", "wv_s", "cv_s")}
    for l in range(depth):
        w = _layer_weights(l, w_in, conv_w, sink, p_attn, p_conv, p_mem, w_o, w_mem_k, w_mem_v, ln1_g, ln1_b)
        x1p, wk, wv, cvp, mk, mv = _mixer_prompt(xp, mem_prompt, w, l, ts=1024, alpha=alpha)
        x1s, wks, wvs, cvs = _mixer_sample(xs, ckt, cvt, cache_conv, cmkt, cmvt, w, l, bb=LANES // t, t=t, alpha=alpha)
        outs["wk_p"].append(wk); outs["wv_p"].append(wv); outs["cv_p"].append(cvp)
        outs["mk_p"].append(mk); outs["mv_p"].append(mv)
        outs["wk_s"].append(wks); outs["wv_s"].append(wvs); outs["cv_s"].append(cvs)
        x1p = x1p.reshape(n_p, D_MODEL)
        g2 = ln2_g[l].reshape(1, D_MODEL)
        b2 = ln2_b[l].reshape(1, D_MODEL)
        i = l // 2
        if l % 2 == 0:
            wg, wu, wd = ffn_w_gate[i].astype(BF16), ffn_w_up[i].astype(BF16), ffn_w_down[i].astype(BF16)
            x2p = _ffn_dense(x1p, wg, wu, wd, g2, b2, tm=512, alpha=alpha, name=f"ffn_dense_p_l{l}")
            xs = _ffn_dense(x1s, wg, wu, wd, g2, b2, tm=512, alpha=alpha, name=f"ffn_dense_s_l{l}")
        else:
            x2p, xs = _moe(x1p, x1s, router_w[i], router_b[i], exp_w_gate[i], exp_w_up[i], exp_w_down[i],
                           g2, b2, alpha=alpha, tm_rows=512, tm_expert=256)
        xp = x2p.reshape(b, seq, D_MODEL)

    return (xp, xs.reshape(nb, t, D_MODEL),
            _token_major(jnp.stack(outs["wk_p"]), N_KV_HEADS),
            _token_major(jnp.stack(outs["wv_p"]), N_KV_HEADS),
            jnp.stack(outs["cv_p"]),
            _token_major(jnp.stack(outs["mk_p"]), MEM_HEADS),
            _token_major(jnp.stack(outs["mv_p"]), MEM_HEADS),
            _token_major(jnp.stack(outs["wk_s"]), N_KV_HEADS),
            _token_major(jnp.stack(outs["wv_s"]), N_KV_HEADS),
            jnp.stack(outs["cv_s"]))
```
